```python
import jax, jax.numpy as jnp
from jax import lax
import numpy as np

D_MODEL = 1024
BATCH = 32
SEQ = 2048
DEPTH = 2

GRID_W = 64
CTX_LEN = 256

RWKV_WIDTH = D_MODEL // 2
RWKV_HEAD_DIM = 64
RWKV_HEADS = RWKV_WIDTH // RWKV_HEAD_DIM
RET_WIDTH = D_MODEL - RWKV_WIDTH
RET_HEAD_DIM = 64
RET_HEADS = RET_WIDTH // RET_HEAD_DIM
MIX_WIDTH = RWKV_WIDTH + RET_WIDTH
DECAY_LORA = 64
ICLR_LORA = 64
VRES_LORA = 32
GATE_LORA = 160
RWKV_COLS = 3 * RWKV_WIDTH + DECAY_LORA + ICLR_LORA + GATE_LORA
RWKV_SPLITS = (RWKV_WIDTH, 2 * RWKV_WIDTH, 3 * RWKV_WIDTH, 3 * RWKV_WIDTH + DECAY_LORA, 3 * RWKV_WIDTH + DECAY_LORA + ICLR_LORA)
N_IN = RWKV_COLS + 5 * RET_WIDTH
FFN_HIDDEN = ((8 * D_MODEL) // 3 + 255) // 256 * 256
RET_CHUNK = 128
ROPE_BASE = 10000.0
NORM_EPS = 1e-6
RWKV_GN_EPS = 64e-5
RET_GN_EPS = 1e-5

kernel_name = 'hymba_rwkv7_retnet_prefix_dit'


def rms_norm(x, g):
    x32 = x.astype(jnp.float32)
    y = x32 * lax.rsqrt(jnp.mean(x32 * x32, axis=-1, keepdims=True) + NORM_EPS)
    return y.astype(x.dtype) * g


def modulate(h, shift, scale):
    return h * (1.0 + scale) + shift


def split_heads(t, n_heads):
    return t.reshape(t.shape[:-1] + (n_heads, t.shape[-1] // n_heads))


def group_norm_heads(y, w, b, eps):
    y32 = y.astype(jnp.float32)
    mu = jnp.mean(y32, axis=-1, keepdims=True)
    var = jnp.mean(jnp.square(y32 - mu), axis=-1, keepdims=True)
    yn = ((y32 - mu) * lax.rsqrt(var + eps)).reshape(y.shape[:-2] + (-1,))
    return yn * w + b


def centred_shift_1d(u):
    h = u.shape[-1] // 2
    prev = jnp.pad(u[:, :-1, :h], ((0, 0), (1, 0), (0, 0)))
    nxt = jnp.pad(u[:, 1:, h:], ((0, 0), (0, 1), (0, 0)))
    return jnp.concatenate([prev, nxt], axis=-1)


def quad_shift_2d(u, rows):
    b, l, ch = u.shape
    q = ch // 4
    g = u.reshape(b, rows, GRID_W, ch)
    from_left = jnp.pad(g[:, :, :-1, :q], ((0, 0), (0, 0), (1, 0), (0, 0)))
    from_right = jnp.pad(g[:, :, 1:, q:2 * q], ((0, 0), (0, 0), (0, 1), (0, 0)))
    from_up = jnp.pad(g[:, :-1, :, 2 * q:3 * q], ((0, 0), (1, 0), (0, 0), (0, 0)))
    from_down = jnp.pad(g[:, 1:, :, 3 * q:], ((0, 0), (0, 1), (0, 0), (0, 0)))
    return jnp.concatenate([from_left, from_right, from_up, from_down], axis=-1).reshape(b, l, ch)


def lerp(u, shifted, mu):
    return u + (shifted - u) * mu


def latent_rope(rows):
    row = jnp.repeat(jnp.arange(rows, dtype=jnp.float32), GRID_W)
    col = jnp.tile(jnp.arange(GRID_W, dtype=jnp.float32), rows)
    nf = RET_HEAD_DIM // 4
    freqs = ROPE_BASE ** (-jnp.arange(nf, dtype=jnp.float32) / nf)
    ang = jnp.concatenate([row[:, None] * freqs, col[:, None] * freqs], axis=-1)
    return jnp.cos(ang)[None, :, None, :], jnp.sin(ang)[None, :, None, :]


def apply_rope(t, cos, sin):
    half = t.shape[-1] // 2
    t1, t2 = t[..., :half], t[..., half:]
    return jnp.concatenate([t1 * cos - t2 * sin, t1 * sin + t2 * cos], axis=-1)


def retention_log_gammas():
    return jnp.log(1.0 - 2.0 ** (-5.0 - jnp.arange(RET_HEADS, dtype=jnp.float32)))


def rwkv7_scan(r, w, k, v, kk, a, s0, reverse):
    def step(S, inp):
        r_t, w_t, k_t, v_t, kk_t, a_t = inp
        sa = jnp.einsum('bhij,bhj->bhi', S, -kk_t)
        S = S * w_t[:, :, None, :] + sa[..., None] * (kk_t * a_t)[:, :, None, :] + v_t[..., None] * k_t[:, :, None, :]
        return S, jnp.einsum('bhij,bhj->bhi', S, r_t)
    xs = tuple(jnp.moveaxis(t, 1, 0) for t in (r, w, k, v, kk, a))
    s_final, y = lax.scan(step, s0, xs, reverse=reverse)
    return jnp.moveaxis(y, 0, 1), s_final


def rwkv7_mix(r, k, v, wd, ad, gd, s0, w0, w_up, a0, a_up, g_up, k_k, k_a, r_k, ln_w, ln_b):
    out_dtype = r.dtype
    r, k, v, wd, ad, gd = (t.astype(jnp.float32) for t in (r, k, v, wd, ad, gd))
    H = RWKV_HEADS
    rh, kh, vh = split_heads(r, H), split_heads(k, H), split_heads(v, H)
    kk = split_heads(k * k_k, H)
    kk = kk * lax.rsqrt(jnp.maximum(jnp.sum(kk * kk, axis=-1, keepdims=True), 1e-24))
    tw = jnp.tanh(wd)
    outs, finals = [], []
    for d in range(2):
        decay = jnp.exp(-jnp.exp(-jax.nn.softplus(-(w0[d] + tw @ w_up[d])) - 0.5))
        iclr = jax.nn.sigmoid(a0[d] + ad @ a_up[d])
        k_dir = k * (1.0 + (iclr - 1.0) * k_a)
        y, s_fin = rwkv7_scan(rh, split_heads(decay, H), split_heads(k_dir, H), vh, kk,
                              split_heads(iclr, H), s0[d], d == 1)
        outs.append(y)
        finals.append(s_fin)
    y = group_norm_heads(outs[0] + outs[1], ln_w, ln_b, RWKV_GN_EPS)
    bonus = (jnp.sum(rh * kh * r_k, axis=-1, keepdims=True) * vh).reshape(y.shape)
    gate = jax.nn.sigmoid(gd) @ g_up
    return ((y + bonus) * gate).astype(out_dtype), jnp.stack(finals)


def retention_chunkwise(q, k, v, log_gamma, s0):
    b, l, h, n = q.shape
    c = RET_CHUNK
    nc = l // c
    idx = jnp.arange(c, dtype=jnp.float32)
    diff = idx[:, None] - idx[None, :]
    inner_decay = jnp.where(diff >= 0, jnp.exp(log_gamma[:, None, None] * jnp.maximum(diff, 0.0)), 0.0)
    query_decay = jnp.exp(log_gamma[None, :] * (idx[:, None] + 1.0))
    key_decay = jnp.exp(log_gamma[None, :] * (c - 1.0 - idx[:, None]))
    chunk_decay = jnp.exp(log_gamma * c)

    def to_chunks(t):
        return jnp.moveaxis(t.reshape(b, nc, c, h, n), 1, 0)

    def step(state, inp):
        qc, kc, vc = inp
        scores = jnp.einsum('bihn,bjhn->bhij', qc, kc) * inner_decay
        inner = jnp.einsum('bhij,bjhn->bihn', scores, vc)
        cross = jnp.einsum('bihn,bhnm->bihm', qc, state) * query_decay[None, :, :, None]
        state = state * chunk_decay[None, :, None, None] + jnp.einsum('bjhn,jh,bjhm->bhnm', kc, key_decay, vc)
        return state, inner + cross

    s_final, out = lax.scan(step, s0, (to_chunks(q), to_chunks(k), to_chunks(v)))
    return jnp.moveaxis(out, 0, 1).reshape(b, l, h, n), s_final


def retnet_mix(q, k, v, g_f, g_b, cos, sin, s0, gn_w, gn_b):
    out_dtype = q.dtype
    H = RET_HEADS
    qh, kh, vh = (split_heads(t.astype(jnp.float32), H) for t in (q, k, v))
    if cos is not None:
        qh, kh = apply_rope(qh, cos, sin), apply_rope(kh, cos, sin)
    kh = kh * RET_HEAD_DIM ** -0.5
    lg = retention_log_gammas()
    o_f, s_f = retention_chunkwise(qh, kh, vh, lg, s0[0])
    o_b, s_b = retention_chunkwise(jnp.flip(qh, 1), jnp.flip(kh, 1), jnp.flip(vh, 1), jnp.flip(lg, 0), s0[1])
    o_b = jnp.flip(o_b, 1)
    y = (group_norm_heads(o_f, gn_w, gn_b, RET_GN_EPS) * jax.nn.silu(g_f.astype(jnp.float32))
         + group_norm_heads(o_b, gn_w, gn_b, RET_GN_EPS) * jax.nn.silu(g_b.astype(jnp.float32)))
    return y.astype(out_dtype), jnp.stack([s_f, s_b])


def swiglu(h, w_in, w_out):
    gate, up = jnp.split(h @ w_in, 2, axis=-1)
    return (jax.nn.silu(gate) * up) @ w_out


def setup_inputs(seed: int = 0) -> dict:
    key = jax.random.key(seed)
    ks = jax.random.split(key, 32)
    f32 = jnp.float32
    D, W = D_MODEL, RWKV_WIDTH

    def nrm(k, shape, scale):
        return jax.random.normal(k, shape, f32) * scale

    return {
        'x': nrm(ks[0], (BATCH, SEQ, D), 1.0),
        'c': nrm(ks[1], (BATCH, D), 1.0),
        'ctx': nrm(ks[2], (BATCH, CTX_LEN, D), 1.0),
        'c_ctx': nrm(ks[3], (D,), 1.0),
        'w_ada': nrm(ks[4], (DEPTH, D, 6 * D), 0.5 * D ** -0.5),
        'b_ada': nrm(ks[5], (DEPTH, 6 * D), 0.02),
        'norm1': 1.0 + nrm(ks[6], (DEPTH, D), 0.02),
        'norm2': 1.0 + nrm(ks[7], (DEPTH, D), 0.02),
        'norm_f': 1.0 + nrm(ks[8], (D,), 0.02),
        'w_in': nrm(ks[9], (DEPTH, D, N_IN), D ** -0.5),
        'w_vres_down': nrm(ks[10], (DEPTH - 1, D, VRES_LORA), D ** -0.5),
        'mu_rwkv': jax.random.uniform(ks[11], (DEPTH, RWKV_COLS), f32),
        'mu_vres': jax.random.uniform(ks[12], (DEPTH - 1, VRES_LORA), f32),
        'w0': jax.random.uniform(ks[13], (DEPTH, 2, W), f32, minval=-6.0, maxval=-1.0),
        'w_up': nrm(ks[14], (DEPTH, 2, DECAY_LORA, W), 0.1),
        'a0': nrm(ks[15], (DEPTH, 2, W), 0.1),
        'a_up': nrm(ks[16], (DEPTH, 2, ICLR_LORA, W), 0.1),
        'g_up': nrm(ks[17], (DEPTH, GATE_LORA, W), GATE_LORA ** -0.5),
        'k_k': 0.85 + nrm(ks[18], (DEPTH, W), 0.02),
        'k_a': 1.0 + nrm(ks[19], (DEPTH, W), 0.02),
        'r_k': nrm(ks[20], (DEPTH, RWKV_HEADS, RWKV_HEAD_DIM), 0.1),
        'v0': 1.0 + nrm(ks[21], (DEPTH - 1, W), 0.1),
        'v_up': nrm(ks[22], (DEPTH - 1, VRES_LORA, W), 0.1),
        'ln_x_w': 1.0 + nrm(ks[23], (DEPTH, W), 0.02),
        'ln_x_b': nrm(ks[24], (DEPTH, W), 0.02),
        'ret_gn_w': 1.0 + nrm(ks[25], (DEPTH, RET_WIDTH), 0.02),
        'ret_gn_b': nrm(ks[26], (DEPTH, RET_WIDTH), 0.02),
        'w_out': nrm(ks[27], (DEPTH, MIX_WIDTH, D), MIX_WIDTH ** -0.5),
        'w_ffn_in': nrm(ks[28], (DEPTH, D, 2 * FFN_HIDDEN), D ** -0.5),
        'w_ffn_out': nrm(ks[29], (DEPTH, FFN_HIDDEN, D), FFN_HIDDEN ** -0.5),
    }


def reference(x, c, ctx, c_ctx, w_ada, b_ada, norm1, norm2, norm_f, w_in, w_vres_down, mu_rwkv, mu_vres,
              w0, w_up, a0, a_up, g_up, k_k, k_a, r_k, v0, v_up, ln_x_w, ln_x_b, ret_gn_w, ret_gn_b,
              w_out, w_ffn_in, w_ffn_out):
    b = x.shape[0]
    ROWS = x.shape[1] // GRID_W
    cos, sin = latent_rope(ROWS)
    silu_c = jax.nn.silu(c)
    silu_cc = jax.nn.silu(c_ctx)
    zero_rwkv = jnp.zeros((2, b, RWKV_HEADS, RWKV_HEAD_DIM, RWKV_HEAD_DIM), jnp.float32)
    zero_ret = jnp.zeros((2, b, RET_HEADS, RET_HEAD_DIM, RET_HEAD_DIM), jnp.float32)
    v_first_c = None
    v_first_l = None
    for i in range(DEPTH):
        last = i == DEPTH - 1
        sh1_l, sc1_l, g1_l, sh2_l, sc2_l, g2_l = jnp.split((silu_c @ w_ada[i] + b_ada[i])[:, None, :], 6, axis=-1)
        sh1_c, sc1_c, g1_c, sh2_c, sc2_c, g2_c = jnp.split(silu_cc @ w_ada[i] + b_ada[i], 6, axis=-1)

        w_proj = w_in[i] if i == 0 else jnp.concatenate([w_in[i], w_vres_down[i - 1]], axis=1)
        p_c = modulate(rms_norm(ctx, norm1[i]), sh1_c, sc1_c) @ w_proj
        p_l = modulate(rms_norm(x, norm1[i]), sh1_l, sc1_l) @ w_proj

        u_c = p_c[..., :RWKV_COLS]
        u_l = p_l[..., :RWKV_COLS]
        u_c = lerp(u_c, centred_shift_1d(u_c), mu_rwkv[i])
        u_l = lerp(u_l, quad_shift_2d(u_l, ROWS), mu_rwkv[i])
        r_c, k_c, v_c, wd_c, ad_c, gd_c = jnp.split(u_c, RWKV_SPLITS, axis=-1)
        r_l, k_l, v_l, wd_l, ad_l, gd_l = jnp.split(u_l, RWKV_SPLITS, axis=-1)
        if i == 0:
            v_first_c, v_first_l = v_c, v_l
        else:
            vr_c = p_c[..., N_IN:]
            vr_l = p_l[..., N_IN:]
            vr_c = lerp(vr_c, centred_shift_1d(vr_c), mu_vres[i - 1])
            vr_l = lerp(vr_l, quad_shift_2d(vr_l, ROWS), mu_vres[i - 1])
            v_c = v_c + (v_first_c - v_c) * jax.nn.sigmoid(v0[i - 1] + vr_c @ v_up[i - 1])
            v_l = v_l + (v_first_l - v_l) * jax.nn.sigmoid(v0[i - 1] + vr_l @ v_up[i - 1])
        rwkv_p = (w0[i], w_up[i], a0[i], a_up[i], g_up[i], k_k[i], k_a[i], r_k[i], ln_x_w[i], ln_x_b[i])
        o_rwkv_c, st_rwkv = rwkv7_mix(r_c, k_c, v_c, wd_c, ad_c, gd_c, zero_rwkv, *rwkv_p)
        o_rwkv_l, _ = rwkv7_mix(r_l, k_l, v_l, wd_l, ad_l, gd_l, st_rwkv, *rwkv_p)

        q_c, kr_c, vr2_c, gf_c, gb_c = jnp.split(p_c[..., RWKV_COLS:N_IN], 5, axis=-1)
        q_l, kr_l, vr2_l, gf_l, gb_l = jnp.split(p_l[..., RWKV_COLS:N_IN], 5, axis=-1)
        o_ret_c, st_ret = retnet_mix(q_c, kr_c, vr2_c, gf_c, gb_c, None, None, zero_ret, ret_gn_w[i], ret_gn_b[i])
        o_ret_l, _ = retnet_mix(q_l, kr_l, vr2_l, gf_l, gb_l, cos, sin, st_ret, ret_gn_w[i], ret_gn_b[i])

        x = x + g1_l * (jnp.concatenate([o_rwkv_l, o_ret_l], axis=-1) @ w_out[i])
        x = x + g2_l * swiglu(modulate(rms_norm(x, norm2[i]), sh2_l, sc2_l), w_ffn_in[i], w_ffn_out[i])
        if not last:
            ctx = ctx + g1_c * (jnp.concatenate([o_rwkv_c, o_ret_c], axis=-1) @ w_out[i])
            ctx = ctx + g2_c * swiglu(modulate(rms_norm(ctx, norm2[i]), sh2_c, sc2_c), w_ffn_in[i], w_ffn_out[i])
    return rms_norm(x, norm_f)
```

```python
import functools

import numpy as np
import jax
import jax.numpy as jnp
from jax import lax
from jax.experimental import pallas as pl
from jax.experimental.pallas import tpu as pltpu

F32 = jnp.float32
BF16 = jnp.bfloat16

GRID_W = 64
HEAD_DIM = 64
PAIR = 2 * HEAD_DIM
RWKV_CHUNK = 64
RET_CHUNK = 128
ROPE_BASE = 10000.0
NORM_EPS = 1e-6
RWKV_GN_EPS = 64e-5
RET_GN_EPS = 1e-5
VMEM_LIMIT = 56 * 1024 * 1024
TM_PROJ = 512
TM_PREP = 256
TM_FFN = 256

NN = ((1,), (0,))
NT = ((1,), (1,))


def _dot(a, b, dims=NN):
    return lax.dot_general(a, b, (dims, ((), ())), preferred_element_type=F32)


def _split(a):
    hi = a.astype(BF16)
    lo = (a - hi.astype(F32)).astype(BF16)
    return hi, lo


def _mm(a, b, dims=NN, passes=1):
    if passes == 1:
        return _dot(a.astype(BF16), b.astype(BF16), dims)
    ah, al = _split(a)
    bh, bl = _split(b)
    return _dot(ah, bh, dims) + (_dot(ah, bl, dims) + _dot(al, bh, dims))


def _mm_exact_lhs(a_bf16, b, pieces=3):
    acc = None
    rem = b
    for _ in range(pieces):
        p = rem.astype(BF16)
        t = _dot(a_bf16, p)
        acc = t if acc is None else acc + t
        rem = rem - p.astype(F32)
    return acc


def _mm_exact_rhs(a, b_bf16, pieces=3):
    acc = None
    rem = a
    for _ in range(pieces):
        p = rem.astype(BF16)
        t = _dot(p, b_bf16)
        acc = t if acc is None else acc + t
        rem = rem - p.astype(F32)
    return acc


def _sigmoid(x):
    return 1.0 / (1.0 + jnp.exp(-x))


def _silu(x):
    return x * _sigmoid(x)


def _softplus(z):
    return jnp.maximum(z, 0.0) + jnp.log(1.0 + jnp.exp(-jnp.abs(z)))


def _params(*sem):
    return pltpu.CompilerParams(dimension_semantics=sem, vmem_limit_bytes=VMEM_LIMIT)


def _ada_kernel(c_ref, w_ref, b_ref, o_ref):
    o_ref[0] = _mm(_silu(c_ref[...]), w_ref[0], passes=3) + b_ref[0]


def _ada(cond, w_ada, b_ada):
    depth, d, n = w_ada.shape
    m = cond.shape[0]
    tn = n // 4
    return pl.pallas_call(
        _ada_kernel,
        grid=(depth, n // tn),
        in_specs=[pl.BlockSpec((m, d), lambda i, j: (0, 0)),
                  pl.BlockSpec((1, d, tn), lambda i, j: (i, 0, j)),
                  pl.BlockSpec((1, 1, tn), lambda i, j: (i, 0, j))],
        out_specs=pl.BlockSpec((1, m, tn), lambda i, j: (i, 0, j)),
        out_shape=jax.ShapeDtypeStruct((depth, m, n), F32),
        compiler_params=_params("arbitrary", "arbitrary"),
        name="ada",
    )(cond, w_ada, b_ada.reshape(depth, 1, n))


def _modnorm(x, g, shift, scale):
    ms = jnp.mean(x * x, axis=-1, keepdims=True)
    y = (x * lax.rsqrt(ms + NORM_EPS)) * g
    return y * (1.0 + scale) + shift


def _in_proj_kernel(x_ref, g_ref, sh_ref, sc_ref, w_ref, *out_refs, widths, tn):
    h = _modnorm(x_ref[0], g_ref[...], sh_ref[0], sc_ref[0]).astype(BF16)
    col = 0
    for o_ref, width in zip(out_refs, widths):
        for c0 in range(0, width, tn):
            c1 = min(c0 + tn, width)
            o_ref[0, :, c0:c1] = _dot(h, w_ref[:, col + c0:col + c1])
        col += width


def _in_proj(x, g, shift, scale, w, widths, tm):
    b, l, d = x.shape
    bm = shift.shape[0]
    mod_map = (lambda i, j: (i, 0, 0)) if bm > 1 else (lambda i, j: (0, 0, 0))
    kern = functools.partial(_in_proj_kernel, widths=widths, tn=512)
    return pl.pallas_call(
        kern,
        grid=(b, l // tm),
        in_specs=[pl.BlockSpec((1, tm, d), lambda i, j: (i, j, 0)),
                  pl.BlockSpec((1, d), lambda i, j: (0, 0)),
                  pl.BlockSpec((1, 1, d), mod_map),
                  pl.BlockSpec((1, 1, d), mod_map),
                  pl.BlockSpec(w.shape, lambda i, j: (0, 0))],
        out_specs=[pl.BlockSpec((1, tm, wd), lambda i, j: (i, j, 0)) for wd in widths],
        out_shape=[jax.ShapeDtypeStruct((b, l, wd), F32) for wd in widths],
        compiler_params=_params("arbitrary", "arbitrary"),
        name="in_proj",
    )(x, g, shift, scale, w)


def _direction_segments(n_cols, parts, offset, width):
    q = n_cols // parts
    return [(offset + i * q, offset + (i + 1) * q, i) for i in range(parts)]


def _shift_lerp(x, mu_ref, shifted_fn, segments, n_cols):
    outs = []
    for g0 in range(0, n_cols, 128):
        xg = x[:, g0:g0 + 128]
        lane = lax.broadcasted_iota(jnp.int32, xg.shape, 1) + g0
        s = None
        for (c0, c1, d) in segments:
            lo, hi = max(c0, g0), min(c1, g0 + 128)
            if lo >= hi:
                continue
            sd = shifted_fn(xg, d, g0)
            if lo == g0 and hi == g0 + 128:
                s = sd
            else:
                part = jnp.where((lane >= lo) & (lane < hi), sd, 0.0)
                s = part if s is None else s + part
        if s is None:
            s = jnp.zeros_like(xg)
        outs.append(xg + (s - xg) * mu_ref[:, g0:g0 + 128])
    return jnp.concatenate(outs, axis=1)


def _rwkv_prep_kernel(*refs, quad, has_vres, segments, width, n_cols):
    it = iter(refs)
    cur_ref = next(it)
    prev_ref = next(it) if quad else None
    next_ref = next(it) if quad else None
    vfirst_ref = next(it) if has_vres else None
    (mu_ref, w0_ref, wup_ref, a0_ref, aup_ref, gup_ref, vup_ref, v0_ref, kk_ref, ka_ref, rk_ref,
     ones_ref) = (next(it) for _ in range(12))
    (r_o, v_o, kk_o, bonus_o, gate_o, lwf_o, lwb_o, kf_o, kb_o, bf_o, bb_o) = (next(it) for _ in range(11))

    x = cur_ref[0]
    t = x.shape[0]
    row = lax.broadcasted_iota(jnp.int32, (t, 128), 0)
    if quad:
        gcol = row % GRID_W
        first = pl.program_id(1) == 0
        last = pl.program_id(1) == pl.num_programs(1) - 1

        def shifted(xg, d, g0):
            if d == 0:
                return jnp.where(gcol == 0, 0.0, pltpu.roll(xg, 1, 0))
            if d == 1:
                return jnp.where(gcol == GRID_W - 1, 0.0, pltpu.roll(xg, t - 1, 0))
            if d == 2:
                halo = jnp.where(first, 0.0, prev_ref[0, :, g0:g0 + 128])
                return jnp.concatenate([halo, xg[:t - GRID_W]], axis=0)
            halo = jnp.where(last, 0.0, next_ref[0, :, g0:g0 + 128])
            return jnp.concatenate([xg[GRID_W:], halo], axis=0)
    else:
        def shifted(xg, d, g0):
            if d == 0:
                return jnp.where(row == 0, 0.0, pltpu.roll(xg, 1, 0))
            return jnp.where(row == t - 1, 0.0, pltpu.roll(xg, t - 1, 0))

    u = _shift_lerp(x, mu_ref, shifted, segments, n_cols)
    w = width
    r, k, v = u[:, :w], u[:, w:2 * w], u[:, 2 * w:3 * w]
    lo_wa = u[:, 3 * w:3 * w + 128]
    lo_gv = u[:, 3 * w + 128:3 * w + 384]

    logw = -jnp.exp(-_softplus(-(w0_ref[...] + _mm(jnp.tanh(lo_wa), wup_ref[...], passes=3))) - 0.5)
    iclr = _sigmoid(a0_ref[...] + _mm(lo_wa, aup_ref[...], passes=3))
    gate = _mm(_sigmoid(lo_gv), gup_ref[...], passes=3)
    if has_vres:
        mix = _sigmoid(v0_ref[...] + _mm(lo_gv, vup_ref[...], passes=3))
        v = v + (vfirst_ref[0] - v) * mix

    ones = ones_ref[...]
    kk = k * kk_ref[...]
    kk = kk * lax.rsqrt(jnp.maximum(_mm_exact_rhs(kk * kk, ones), 1e-24))
    bonus = _mm_exact_rhs(r * k * rk_ref[...], ones) * v

    r_o[0], v_o[0], kk_o[0], bonus_o[0], gate_o[0] = r, v, kk, bonus, gate
    ka = ka_ref[...]
    for d, (lw_o, k_o, b_o) in enumerate(((lwf_o, kf_o, bf_o), (lwb_o, kb_o, bb_o))):
        a = iclr[:, d * w:(d + 1) * w]
        lw_o[0] = logw[:, d * w:(d + 1) * w]
        k_o[0] = k * (1.0 + (a - 1.0) * ka)
        b_o[0] = kk * a


def _rwkv_prep(p, vfirst, consts, quad, tm):
    b, l, n_cols = p.shape
    w = consts["k_k"].shape[-1]
    has_vres = vfirst is not None
    n_rwkv = consts["n_rwkv"]
    n_vres = consts["n_vres"]
    parts = 4 if quad else 2
    segments = _direction_segments(n_rwkv, parts, 0, w) + _direction_segments(n_vres, parts, n_rwkv, w)
    kern = functools.partial(_rwkv_prep_kernel, quad=quad, has_vres=has_vres, segments=segments,
                             width=w, n_cols=n_cols)
    rows = tm // GRID_W
    n_rows = l // GRID_W
    in_specs = [pl.BlockSpec((1, tm, n_cols), lambda i, j: (i, j, 0))]
    args = [p]
    if quad:
        in_specs += [
            pl.BlockSpec((1, GRID_W, n_cols), lambda i, j: (i, jnp.maximum(j * rows - 1, 0), 0)),
            pl.BlockSpec((1, GRID_W, n_cols), lambda i, j: (i, jnp.minimum((j + 1) * rows, n_rows - 1), 0)),
        ]
        args += [p, p]
    if has_vres:
        in_specs.append(pl.BlockSpec((1, tm, w), lambda i, j: (i, j, 0)))
        args.append(vfirst)
    for name in ("mu", "w0", "w_up", "a0", "a_up", "g_up", "v_up", "v0", "k_k", "k_a", "r_k", "ones"):
        arr = consts[name]
        in_specs.append(pl.BlockSpec(arr.shape, lambda i, j: (0, 0)))
        args.append(arr)
    return pl.pallas_call(
        kern,
        grid=(b, l // tm),
        in_specs=in_specs,
        out_specs=[pl.BlockSpec((1, tm, w), lambda i, j: (i, j, 0))] * 11,
        out_shape=[jax.ShapeDtypeStruct((b, l, w), F32)] * 11,
        compiler_params=_params("arbitrary", "arbitrary"),
        name="rwkv_prep",
    )(*args)


def _block_diag(y):
    lane = lax.broadcasted_iota(jnp.int32, y.shape, 1)
    return jnp.concatenate([jnp.where(lane < HEAD_DIM, y, 0.0), jnp.where(lane >= HEAD_DIM, y, 0.0)], axis=0)


def _head_diag_mask():
    r = lax.broadcasted_iota(jnp.int32, (PAIR, PAIR), 0)
    c = lax.broadcasted_iota(jnp.int32, (PAIR, PAIR), 1)
    return (r < HEAD_DIM) == (c < HEAD_DIM)


def _rwkv_chunk(r, v, kk, lw, kd, bd, s_ref, y_ref, reverse, passes):
    c = r.shape[0]
    n_pairs = r.shape[1] // PAIR
    t_idx = lax.broadcasted_iota(jnp.int32, (c, c), 0)
    s_idx = lax.broadcasted_iota(jnp.int32, (c, c), 1)
    tri = jnp.where((s_idx >= t_idx) if reverse else (s_idx <= t_idx), 1.0, 0.0).astype(BF16)
    lc = _mm_exact_lhs(tri, lw)
    l_end = lc[0:1] if reverse else lc[c - 1:c]
    e_inv = jnp.exp(-lc)
    r_t = r * jnp.exp(lc)
    a_t = -kk * jnp.exp(lc - lw)
    k_t = kd * e_inv
    b_t = bd * e_inv
    to_end = jnp.exp(l_end - lc)
    k_hat = kd * to_end
    b_hat = bd * to_end
    p_end = jnp.exp(l_end)

    row = lax.broadcasted_iota(jnp.int32, (c, PAIR), 0)
    col = lax.broadcasted_iota(jnp.int32, (c, PAIR), 1) % HEAD_DIM
    before = (col > row) if reverse else (col < row)
    upto = (col >= row) if reverse else (col <= row)
    eye = jnp.where(col == row, 1.0, 0.0)
    diag = _head_diag_mask()

    for p in range(n_pairs):
        sl = slice(p * PAIR, (p + 1) * PAIR)
        z = jnp.concatenate([a_t[:, sl], r_t[:, sl]], axis=0)
        gb = _mm(z, _block_diag(b_t[:, sl]), NT, passes)
        gk = _mm(z, _block_diag(k_t[:, sl]), NT, passes)
        l_ab = jnp.where(before, gb[:c], 0.0)
        m_rb = jnp.where(upto, gb[c:], 0.0)
        l_ak = jnp.where(before, gk[:c], 0.0)
        m_rk = jnp.where(upto, gk[c:], 0.0)

        inv = eye + l_ab
        power = l_ab
        n_sq = int(np.log2(c)) - 1
        for step in range(n_sq):
            prod = _mm(jnp.concatenate([power, inv], axis=0) if step else power,
                       _block_diag(power), NN, passes)
            if step:
                inv = inv + prod[c:]
                power = prod[:c]
            else:
                power = prod
        inv = inv + _mm(inv, _block_diag(power), NN, passes)

        s0 = s_ref[p]
        zs = _mm(z, s0, NT, passes)
        v_p = v[:, sl]
        kv = _mm(jnp.concatenate([l_ak, m_rk], axis=0), _block_diag(v_p), NN, passes)
        u = _mm(inv, _block_diag(zs[:c] + kv[:c]), NN, passes)
        y = zs[c:] + kv[c:] + _mm(m_rb, _block_diag(u), NN, passes)
        y_ref[0, :, sl] = y

        uv_t = jnp.concatenate([u, v_p], axis=0).T
        upd = _mm(uv_t, jnp.concatenate([b_hat[:, sl], k_hat[:, sl]], axis=0), NN, passes)
        s_ref[p] = s0 * p_end[:, sl] + jnp.where(diag, upd, 0.0)


def _rwkv_scan_kernel(rf, vf, kkf, lwf, kf, bf, rb, vb, kkb, lwb, kb, bb, s0_ref,
                      yf_ref, yb_ref, sout_ref, s_scr, *, passes):
    i = pl.program_id(1)

    @pl.when(i == 0)
    def _():
        s_scr[...] = s0_ref[:, 0]

    _rwkv_chunk(rf[0], vf[0], kkf[0], lwf[0], kf[0], bf[0], s_scr.at[0], yf_ref, False, passes)
    _rwkv_chunk(rb[0], vb[0], kkb[0], lwb[0], kb[0], bb[0], s_scr.at[1], yb_ref, True, passes)

    @pl.when(i == pl.num_programs(1) - 1)
    def _():
        sout_ref[:, 0] = s_scr[...]


def _rwkv_scan(r, v, kk, lwf, lwb, kf, kb, bf, bb, s0, passes=3):
    b, l, w = r.shape
    c = RWKV_CHUNK
    nc = l // c
    n_pairs = w // PAIR
    fwd = pl.BlockSpec((1, c, w), lambda i, j: (i, j, 0))
    bwd = pl.BlockSpec((1, c, w), lambda i, j: (i, nc - 1 - j, 0))
    st = pl.BlockSpec((2, 1, n_pairs, PAIR, PAIR), lambda i, j: (0, i, 0, 0, 0))
    return pl.pallas_call(
        functools.partial(_rwkv_scan_kernel, passes=passes),
        grid=(b, nc),
        in_specs=[fwd] * 6 + [bwd] * 6 + [st],
        out_specs=[fwd, bwd, st],
        out_shape=[jax.ShapeDtypeStruct((b, l, w), F32)] * 2 + [jax.ShapeDtypeStruct(s0.shape, F32)],
        scratch_shapes=[pltpu.VMEM((2, n_pairs, PAIR, PAIR), F32)],
        compiler_params=_params("arbitrary", "arbitrary"),
        name="rwkv_scan",
    )(r, v, kk, lwf, kf, bf, r, v, kk, lwb, kb, bb, s0)


def _rope(x, cos, sin_signed):
    w = x.shape[1]
    half = HEAD_DIM // 2
    lane = lax.broadcasted_iota(jnp.int32, x.shape, 1) % HEAD_DIM
    rot = jnp.where(lane < half, pltpu.roll(x, w - half, 1), pltpu.roll(x, half, 1))
    return x * cos + rot * sin_signed


def _ret_chunk(q, k, v, inner_ref, qdec, kdec, cdec, s_ref, o_ref, scale):
    n_pairs = q.shape[1] // PAIR
    kq = k * scale
    diag = _head_diag_mask()
    for p in range(n_pairs):
        sl = slice(p * PAIR, (p + 1) * PAIR)
        q_p, k_p, v_p = q[:, sl], kq[:, sl], v[:, sl]
        scores = _mm(q_p, _block_diag(k_p), NT) * inner_ref[p]
        s0 = s_ref[p]
        inner = _mm(scores, _block_diag(v_p))
        cross = _mm(q_p, s0) * qdec[:, sl]
        o_ref[0, :, sl] = inner + cross
        upd = _mm((k_p * kdec[:, sl]).T, v_p)
        s_ref[p] = s0 * cdec[:, sl] + jnp.where(diag, upd, 0.0)


def _ret_scan_kernel(qkvf, cosf, sinf, qkvb, cosb, sinb, inf_ref, inb_ref, dec_ref, s0_ref,
                     of_ref, ob_ref, sout_ref, s_scr, *, width):
    i = pl.program_id(1)

    @pl.when(i == 0)
    def _():
        s_scr[...] = s0_ref[:, 0]

    scale = HEAD_DIM ** -0.5
    w = width
    for d, (qkv, cos, sin, inner_ref, o_ref) in enumerate(
            ((qkvf, cosf, sinf, inf_ref, of_ref), (qkvb, cosb, sinb, inb_ref, ob_ref))):
        x = qkv[0]
        q = _rope(x[:, :w], cos[...], sin[...])
        k = _rope(x[:, w:2 * w], cos[...], sin[...])
        _ret_chunk(q, k, x[:, 2 * w:], inner_ref, dec_ref[d, 0], dec_ref[d, 1], dec_ref[d, 2, 0:1],
                   s_scr.at[d], o_ref, scale)

    @pl.when(i == pl.num_programs(1) - 1)
    def _():
        sout_ref[:, 0] = s_scr[...]


def _ret_scan(qkv, cos, sin, tables, s0):
    b, l, w3 = qkv.shape
    w = w3 // 3
    c = RET_CHUNK
    nc = l // c
    n_pairs = w // PAIR
    inner_f, inner_b, dec = tables
    st = pl.BlockSpec((2, 1, n_pairs, PAIR, PAIR), lambda i, j: (0, i, 0, 0, 0))
    const3 = pl.BlockSpec(inner_f.shape, lambda i, j: (0, 0, 0))
    return pl.pallas_call(
        functools.partial(_ret_scan_kernel, width=w),
        grid=(b, nc),
        in_specs=[pl.BlockSpec((1, c, w3), lambda i, j: (i, j, 0)),
                  pl.BlockSpec((c, w), lambda i, j: (j, 0)),
                  pl.BlockSpec((c, w), lambda i, j: (j, 0)),
                  pl.BlockSpec((1, c, w3), lambda i, j: (i, nc - 1 - j, 0)),
                  pl.BlockSpec((c, w), lambda i, j: (nc - 1 - j, 0)),
                  pl.BlockSpec((c, w), lambda i, j: (nc - 1 - j, 0)),
                  const3, const3,
                  pl.BlockSpec(dec.shape, lambda i, j: (0, 0, 0, 0)),
                  st],
        out_specs=[pl.BlockSpec((1, c, w), lambda i, j: (i, j, 0)),
                   pl.BlockSpec((1, c, w), lambda i, j: (i, nc - 1 - j, 0)),
                   st],
        out_shape=[jax.ShapeDtypeStruct((b, l, w), F32)] * 2 + [jax.ShapeDtypeStruct(s0.shape, F32)],
        scratch_shapes=[pltpu.VMEM((2, n_pairs, PAIR, PAIR), F32)],
        compiler_params=_params("arbitrary", "arbitrary"),
        name="ret_scan",
    )(qkv, cos, sin, qkv, cos, sin, inner_f, inner_b, dec, s0)


def _ret_tables(n_heads):
    c = RET_CHUNK
    lg = np.log(1.0 - 2.0 ** (-5.0 - np.arange(n_heads, dtype=np.float64)))
    idx = np.arange(c, dtype=np.float64)
    diff = idx[:, None] - idx[None, :]

    def inner(log_gamma, reverse):
        dd = -diff if reverse else diff
        m = np.where(dd >= 0, np.exp(log_gamma[:, None, None] * np.maximum(dd, 0.0)), 0.0)
        return m.reshape(n_heads // 2, 2, c, c).transpose(0, 2, 1, 3).reshape(n_heads // 2, c, 2 * c)

    def per_lane(tab):
        return np.repeat(tab, HEAD_DIM, axis=1)

    dec = []
    for reverse in (False, True):
        g = lg[::-1] if reverse else lg
        pos = (c - 1.0 - idx) if reverse else idx
        qd = per_lane(np.exp(g[None, :] * (pos[:, None] + 1.0)))
        kd = per_lane(np.exp(g[None, :] * (c - 1.0 - pos[:, None])))
        cd = per_lane(np.broadcast_to(np.exp(g * c)[None, :], (c, n_heads)))
        dec.append(np.stack([qd, kd, cd]))
    return (jnp.asarray(inner(lg, False), F32), jnp.asarray(inner(lg[::-1], True), F32),
            jnp.asarray(np.stack(dec), F32))


def _rope_tables(l, n_heads, rotate):
    if not rotate:
        return jnp.ones((l, n_heads * HEAD_DIM), F32), jnp.zeros((l, n_heads * HEAD_DIM), F32)
    rows = l // GRID_W
    row = jnp.repeat(jnp.arange(rows, dtype=F32), GRID_W)
    col = jnp.tile(jnp.arange(GRID_W, dtype=F32), rows)
    nf = HEAD_DIM // 4
    freqs = ROPE_BASE ** (-jnp.arange(nf, dtype=F32) / nf)
    ang = jnp.concatenate([row[:, None] * freqs, col[:, None] * freqs], axis=-1)
    cos = jnp.tile(jnp.cos(ang), (1, 2 * n_heads))
    sin = jnp.tile(jnp.concatenate([-jnp.sin(ang), jnp.sin(ang)], axis=-1), (1, n_heads))
    return cos, sin


def _group_norm(y, ones, w, b, eps):
    inv_n = 1.0 / HEAD_DIM
    mu = _dot(y.astype(BF16), ones) * inv_n
    d = y - mu
    var = _dot((d * d).astype(BF16), ones) * inv_n
    return d * lax.rsqrt(var + eps) * w + b


def _out_ffn_kernel(x_ref, yf_ref, yb_ref, bonus_ref, gate_ref, of_ref, ob_ref, pg_ref,
                    lnw_ref, lnb_ref, gnw_ref, gnb_ref, ones_ref, wo_ref,
                    g1_ref, sh2_ref, sc2_ref, g2_ref, n2_ref, nf_ref, wfi_ref, wfo_ref,
                    o_ref, *, width, hidden, th, final_norm):
    ones = ones_ref[...]
    w = width
    o_rwkv = (_group_norm(yf_ref[0] + yb_ref[0], ones, lnw_ref[...], lnb_ref[...], RWKV_GN_EPS)
              + bonus_ref[0]) * gate_ref[0]
    pg = pg_ref[0]
    o_ret = (_group_norm(of_ref[0], ones, gnw_ref[...], gnb_ref[...], RET_GN_EPS) * _silu(pg[:, :w])
             + _group_norm(ob_ref[0], ones, gnw_ref[...], gnb_ref[...], RET_GN_EPS) * _silu(pg[:, w:]))
    attn = _dot(o_rwkv.astype(BF16), wo_ref[:w]) + _dot(o_ret.astype(BF16), wo_ref[w:])
    x1 = x_ref[0] + g1_ref[0] * attn
    h = _modnorm(x1, n2_ref[...], sh2_ref[0], sc2_ref[0]).astype(BF16)
    acc = None
    for c0 in range(0, hidden, th):
        gate = _dot(h, wfi_ref[:, c0:c0 + th])
        up = _dot(h, wfi_ref[:, hidden + c0:hidden + c0 + th])
        part = _dot((_silu(gate) * up).astype(BF16), wfo_ref[c0:c0 + th])
        acc = part if acc is None else acc + part
    x2 = x1 + g2_ref[0] * acc
    if final_norm:
        ms = jnp.mean(x2 * x2, axis=-1, keepdims=True)
        x2 = (x2 * lax.rsqrt(ms + NORM_EPS)) * nf_ref[...]
    o_ref[0] = x2


def _out_ffn(x, yf, yb, bonus, gate, of, ob, pg, consts, mods, final_norm, tm):
    b, l, d = x.shape
    w = yf.shape[-1]
    hidden = consts["w_ffn_out"].shape[0]
    th = hidden // 2 if (hidden // 2) % 128 == 0 else hidden
    bm = mods[0].shape[0]
    mod_map = (lambda i, j: (i, 0, 0)) if bm > 1 else (lambda i, j: (0, 0, 0))
    tok = lambda width: pl.BlockSpec((1, tm, width), lambda i, j: (i, j, 0))
    const = lambda arr: pl.BlockSpec(arr.shape, lambda i, j: (0,) * arr.ndim, pipeline_mode=pl.Buffered(1))
    names = ("ln_w", "ln_b", "gn_w", "gn_b", "ones", "w_out")
    tail = ("norm2", "norm_f", "w_ffn_in", "w_ffn_out")
    in_specs = ([tok(d)] + [tok(w)] * 6 + [tok(2 * w)] + [const(consts[n]) for n in names]
                + [pl.BlockSpec((1, 1, d), mod_map)] * 4 + [const(consts[n]) for n in tail])
    args = ([x, yf, yb, bonus, gate, of, ob, pg] + [consts[n] for n in names] + list(mods)
            + [consts[n] for n in tail])
    kern = functools.partial(_out_ffn_kernel, width=w, hidden=hidden, th=th, final_norm=final_norm)
    return pl.pallas_call(
        kern,
        grid=(b, l // tm),
        in_specs=in_specs,
        out_specs=tok(d),
        out_shape=jax.ShapeDtypeStruct((b, l, d), F32),
        compiler_params=_params("arbitrary", "arbitrary"),
        name="out_ffn",
    )(*args)


def _pad_rows(a, rows, offset=0):
    out = jnp.zeros((rows,) + a.shape[1:], a.dtype)
    return out.at[offset:offset + a.shape[0]].set(a)


def _layer_consts(i, w_in, w_vres_down, mu_rwkv, mu_vres, w0, w_up, a0, a_up, g_up, k_k, k_a, r_k, v0, v_up,
                  ln_x_w, ln_x_b, ret_gn_w, ret_gn_b, w_out, w_ffn_in, w_ffn_out, norm2, norm_f):
    width = w0.shape[-1]
    n_decay, n_iclr, n_gate = w_up.shape[2], a_up.shape[2], g_up.shape[1]
    n_vres = v_up.shape[1]
    n_rwkv = 3 * width + n_decay + n_iclr + n_gate
    assert n_decay + n_iclr == 128 and n_gate + n_vres <= 256
    rwkv_cols = 3 * width + 128 + 256
    d = w_in.shape[1]
    has_vres = i > 0
    wi = w_in[i]
    pad = jnp.zeros((d, rwkv_cols - n_rwkv - n_vres), F32)
    vres_cols = w_vres_down[i - 1] if has_vres else jnp.zeros((d, n_vres), F32)
    w_proj = jnp.concatenate([wi[:, :n_rwkv], vres_cols, pad, wi[:, n_rwkv:]], axis=1).astype(BF16)
    mu_v = mu_vres[i - 1] if has_vres else jnp.zeros((n_vres,), F32)
    mu = jnp.concatenate([mu_rwkv[i], mu_v, jnp.zeros((rwkv_cols - n_rwkv - n_vres,), F32)])[None]
    cat2 = lambda a: jnp.concatenate([a[0], a[1]], axis=-1)
    head = jnp.arange(width) // HEAD_DIM
    ones = (head[:, None] == head[None, :]).astype(BF16)
    row = lambda a: a.reshape(1, -1)
    return dict(
        w_proj=w_proj, widths=(rwkv_cols, 3 * width, 2 * width), n_rwkv=n_rwkv, n_vres=n_vres, mu=mu,
        w0=row(cat2(w0[i])), w_up=_pad_rows(cat2(w_up[i]), 128), a0=row(cat2(a0[i])),
        a_up=_pad_rows(cat2(a_up[i]), 128, n_decay), g_up=_pad_rows(g_up[i], 256),
        v_up=_pad_rows(v_up[i - 1] if has_vres else jnp.zeros((n_vres, width), F32), 256, n_gate),
        v0=row(v0[i - 1] if has_vres else jnp.zeros((width,), F32)),
        k_k=row(k_k[i]), k_a=row(k_a[i]), r_k=row(r_k[i]), ones=ones,
        ln_w=row(ln_x_w[i]), ln_b=row(ln_x_b[i]), gn_w=row(ret_gn_w[i]), gn_b=row(ret_gn_b[i]),
        w_out=w_out[i].astype(BF16), w_ffn_in=w_ffn_in[i].astype(BF16), w_ffn_out=w_ffn_out[i].astype(BF16),
        norm2=row(norm2[i]), norm_f=row(norm_f),
    )


def _mixers(x, norm1, shift, scale, consts, vfirst, s_rwkv, s_ret, rope, tables, quad, tm):
    p_rwkv, p_qkv, p_g = _in_proj(x, norm1, shift, scale, consts["w_proj"], consts["widths"], tm)
    r, v, kk, bonus, gate, lwf, lwb, kf, kb, bf, bb = _rwkv_prep(
        p_rwkv, vfirst, consts, quad, min(TM_PREP, x.shape[1]) if quad else x.shape[1])
    yf, yb, s_rwkv = _rwkv_scan(r, v, kk, lwf, lwb, kf, kb, bf, bb, s_rwkv)
    of, ob, s_ret = _ret_scan(p_qkv, rope[0], rope[1], tables, s_ret)
    return (yf, yb, bonus, gate, of, ob, p_g), v, s_rwkv, s_ret


def kernel(x, c, ctx, c_ctx, w_ada, b_ada, norm1, norm2, norm_f, w_in, w_vres_down, mu_rwkv, mu_vres, w0, w_up, a0, a_up, g_up, k_k, k_a, r_k, v0, v_up, ln_x_w, ln_x_b, ret_gn_w, ret_gn_b, w_out, w_ffn_in, w_ffn_out):
    b, l, d = x.shape
    l_ctx = ctx.shape[1]
    depth = w_in.shape[0]
    width = w0.shape[-1]
    n_heads = width // HEAD_DIM
    n_pairs = n_heads // 2
    tm = min(TM_PROJ, l)
    tm_ctx = min(TM_PROJ, l_ctx)
    tm_ffn = min(TM_FFN, l)
    tm_ffn_ctx = min(TM_FFN, l_ctx)

    m_pad = -(-(b + 1) // 8) * 8
    cond = jnp.zeros((m_pad, d), F32).at[:b].set(c).at[b].set(c_ctx)
    mod = _ada(cond, w_ada, b_ada)

    tables = _ret_tables(n_heads)
    rope_l = _rope_tables(l, n_heads, True)
    rope_c = _rope_tables(l_ctx, n_heads, False)
    zero_state = jnp.zeros((2, b, n_pairs, PAIR, PAIR), F32)

    vfirst_c = vfirst_l = None
    for i in range(depth):
        last = i == depth - 1
        consts = _layer_consts(i, w_in, w_vres_down, mu_rwkv, mu_vres, w0, w_up, a0, a_up, g_up, k_k, k_a, r_k,
                               v0, v_up, ln_x_w, ln_x_b, ret_gn_w, ret_gn_b, w_out, w_ffn_in, w_ffn_out,
                               norm2, norm_f)
        mods_l = [mod[i, :b, None, j * d:(j + 1) * d] for j in range(6)]
        mods_c = [mod[i, b:b + 1, None, j * d:(j + 1) * d] for j in range(6)]
        n1 = norm1[i][None]

        mix_c, v_c, s_rwkv, s_ret = _mixers(ctx, n1, mods_c[0], mods_c[1], consts, vfirst_c,
                                            zero_state, zero_state, rope_c, tables, False, tm_ctx)
        mix_l, v_l, _, _ = _mixers(x, n1, mods_l[0], mods_l[1], consts, vfirst_l,
                                   s_rwkv, s_ret, rope_l, tables, True, tm)
        if i == 0:
            vfirst_c, vfirst_l = v_c, v_l
        x = _out_ffn(x, *mix_l, consts, (mods_l[2], mods_l[3], mods_l[4], mods_l[5]), last, tm_ffn)
        if not last:
            ctx = _out_ffn(ctx, *mix_c, consts, (mods_c[2], mods_c[3], mods_c[4], mods_c[5]), False, tm_ffn_ctx)
    return x
```

```python
import functools

import numpy as np
import jax
import jax.numpy as jnp
from jax import lax
from jax.experimental import pallas as pl
from jax.experimental.pallas import tpu as pltpu

F32 = jnp.float32
BF16 = jnp.bfloat16

GRID_W = 64
HEAD_DIM = 64
PAIR = 2 * HEAD_DIM
RWKV_CHUNK = 64
RET_CHUNK = 128
ROPE_BASE = 10000.0
NORM_EPS = 1e-6
RWKV_GN_EPS = 64e-5
RET_GN_EPS = 1e-5
VMEM_LIMIT = 56 * 1024 * 1024
TM_PROJ = 512
TM_PREP = 256
TM_FFN = 256

NN = ((1,), (0,))
NT = ((1,), (1,))


def _dot(a, b, dims=NN):
    return lax.dot_general(a, b, (dims, ((), ())), preferred_element_type=F32)


def _split(a):
    hi = a.astype(BF16)
    lo = (a - hi.astype(F32)).astype(BF16)
    return hi, lo


def _mm(a, b, dims=NN, passes=1):
    if passes == 1:
        return _dot(a.astype(BF16), b.astype(BF16), dims)
    ah, al = _split(a)
    bh, bl = _split(b)
    lhs = jnp.concatenate([ah, al, ah], axis=1)
    rhs = jnp.concatenate([bh, bh, bl], axis=0 if dims == NN else 1)
    return _dot(lhs, rhs, dims)


def _mm_exact_lhs(a_bf16, b, pieces=3):
    acc = None
    rem = b
    for _ in range(pieces):
        p = rem.astype(BF16)
        t = _dot(a_bf16, p)
        acc = t if acc is None else acc + t
        rem = rem - p.astype(F32)
    return acc


def _mm_exact_rhs(a, b_bf16, pieces=3):
    acc = None
    rem = a
    for _ in range(pieces):
        p = rem.astype(BF16)
        t = _dot(p, b_bf16)
        acc = t if acc is None else acc + t
        rem = rem - p.astype(F32)
    return acc


def _sigmoid(x):
    return 1.0 / (1.0 + jnp.exp(-x))


def _silu(x):
    return x * _sigmoid(x)


def _softplus(z):
    return jnp.maximum(z, 0.0) + jnp.log(1.0 + jnp.exp(-jnp.abs(z)))


def _params(*sem):
    return pltpu.CompilerParams(dimension_semantics=sem, vmem_limit_bytes=VMEM_LIMIT)


def _ada_kernel(c_ref, w_ref, b_ref, o_ref):
    o_ref[0] = _mm(_silu(c_ref[...]), w_ref[0], passes=3) + b_ref[0]


def _ada(cond, w_ada, b_ada):
    depth, d, n = w_ada.shape
    m = cond.shape[0]
    tn = n // 4
    return pl.pallas_call(
        _ada_kernel,
        grid=(depth, n // tn),
        in_specs=[pl.BlockSpec((m, d), lambda i, j: (0, 0)),
                  pl.BlockSpec((1, d, tn), lambda i, j: (i, 0, j)),
                  pl.BlockSpec((1, 1, tn), lambda i, j: (i, 0, j))],
        out_specs=pl.BlockSpec((1, m, tn), lambda i, j: (i, 0, j)),
        out_shape=jax.ShapeDtypeStruct((depth, m, n), F32),
        compiler_params=_params("arbitrary", "arbitrary"),
        name="ada",
    )(cond, w_ada, b_ada.reshape(depth, 1, n))


def _modnorm(x, g, shift, scale):
    ms = jnp.mean(x * x, axis=-1, keepdims=True)
    y = (x * lax.rsqrt(ms + NORM_EPS)) * g
    return y * (1.0 + scale) + shift


def _in_proj_kernel(x_ref, g_ref, sh_ref, sc_ref, w_ref, *out_refs, widths, tn):
    h = _modnorm(x_ref[0], g_ref[...], sh_ref[0], sc_ref[0]).astype(BF16)
    col = 0
    for o_ref, width in zip(out_refs, widths):
        for c0 in range(0, width, tn):
            c1 = min(c0 + tn, width)
            o_ref[0, :, c0:c1] = _dot(h, w_ref[:, col + c0:col + c1])
        col += width


def _in_proj(x, g, shift, scale, w, widths, tm):
    b, l, d = x.shape
    bm = shift.shape[0]
    mod_map = (lambda i, j: (i, 0, 0)) if bm > 1 else (lambda i, j: (0, 0, 0))
    kern = functools.partial(_in_proj_kernel, widths=widths, tn=512)
    return pl.pallas_call(
        kern,
        grid=(b, l // tm),
        in_specs=[pl.BlockSpec((1, tm, d), lambda i, j: (i, j, 0)),
                  pl.BlockSpec((1, d), lambda i, j: (0, 0)),
                  pl.BlockSpec((1, 1, d), mod_map),
                  pl.BlockSpec((1, 1, d), mod_map),
                  pl.BlockSpec(w.shape, lambda i, j: (0, 0))],
        out_specs=[pl.BlockSpec((1, tm, wd), lambda i, j: (i, j, 0)) for wd in widths],
        out_shape=[jax.ShapeDtypeStruct((b, l, wd), F32) for wd in widths],
        compiler_params=_params("arbitrary", "arbitrary"),
        name="in_proj",
    )(x, g, shift, scale, w)


def _direction_segments(n_cols, parts, offset, width):
    q = n_cols // parts
    return [(offset + i * q, offset + (i + 1) * q, i) for i in range(parts)]


def _shift_lerp(x, mu_ref, shifted_fn, segments, n_cols):
    outs = []
    for g0 in range(0, n_cols, 128):
        xg = x[:, g0:g0 + 128]
        lane = lax.broadcasted_iota(jnp.int32, xg.shape, 1) + g0
        s = None
        for (c0, c1, d) in segments:
            lo, hi = max(c0, g0), min(c1, g0 + 128)
            if lo >= hi:
                continue
            sd = shifted_fn(xg, d, g0)
            if lo == g0 and hi == g0 + 128:
                s = sd
            else:
                part = jnp.where((lane >= lo) & (lane < hi), sd, 0.0)
                s = part if s is None else s + part
        if s is None:
            s = jnp.zeros_like(xg)
        outs.append(xg + (s - xg) * mu_ref[:, g0:g0 + 128])
    return jnp.concatenate(outs, axis=1)


def _rwkv_prep_kernel(*refs, quad, has_vres, segments, width, n_cols):
    it = iter(refs)
    cur_ref = next(it)
    prev_ref = next(it) if quad else None
    next_ref = next(it) if quad else None
    vfirst_ref = next(it) if has_vres else None
    (mu_ref, w0_ref, wup_ref, a0_ref, aup_ref, gup_ref, vup_ref, v0_ref, kk_ref, ka_ref, rk_ref,
     ones_ref) = (next(it) for _ in range(12))
    (r_o, v_o, kk_o, bonus_o, gate_o, lwf_o, lwb_o, kf_o, kb_o, bf_o, bb_o) = (next(it) for _ in range(11))

    x = cur_ref[0]
    t = x.shape[0]
    row = lax.broadcasted_iota(jnp.int32, (t, 128), 0)
    if quad:
        gcol = row % GRID_W
        first = pl.program_id(1) == 0
        last = pl.program_id(1) == pl.num_programs(1) - 1

        def shifted(xg, d, g0):
            if d == 0:
                return jnp.where(gcol == 0, 0.0, pltpu.roll(xg, 1, 0))
            if d == 1:
                return jnp.where(gcol == GRID_W - 1, 0.0, pltpu.roll(xg, t - 1, 0))
            if d == 2:
                halo = jnp.where(first, 0.0, prev_ref[0, :, g0:g0 + 128])
                return jnp.concatenate([halo, xg[:t - GRID_W]], axis=0)
            halo = jnp.where(last, 0.0, next_ref[0, :, g0:g0 + 128])
            return jnp.concatenate([xg[GRID_W:], halo], axis=0)
    else:
        def shifted(xg, d, g0):
            if d == 0:
                return jnp.where(row == 0, 0.0, pltpu.roll(xg, 1, 0))
            return jnp.where(row == t - 1, 0.0, pltpu.roll(xg, t - 1, 0))

    u = _shift_lerp(x, mu_ref, shifted, segments, n_cols)
    w = width
    r, k, v = u[:, :w], u[:, w:2 * w], u[:, 2 * w:3 * w]
    lo_wa = u[:, 3 * w:3 * w + 128]
    lo_gv = u[:, 3 * w + 128:3 * w + 384]

    logw = -jnp.exp(-_softplus(-(w0_ref[...] + _mm(jnp.tanh(lo_wa), wup_ref[...], passes=3))) - 0.5)
    iclr = _sigmoid(a0_ref[...] + _mm(lo_wa, aup_ref[...], passes=3))
    gate = _mm(_sigmoid(lo_gv), gup_ref[...], passes=3)
    if has_vres:
        mix = _sigmoid(v0_ref[...] + _mm(lo_gv, vup_ref[...], passes=3))
        v = v + (vfirst_ref[0] - v) * mix

    ones = ones_ref[...]
    kk = k * kk_ref[...]
    kk = kk * lax.rsqrt(jnp.maximum(_mm_exact_rhs(kk * kk, ones), 1e-24))
    bonus = _mm_exact_rhs(r * k * rk_ref[...], ones) * v

    r_o[0], v_o[0], kk_o[0], bonus_o[0], gate_o[0] = r, v, kk, bonus, gate
    ka = ka_ref[...]
    for d, (lw_o, k_o, b_o) in enumerate(((lwf_o, kf_o, bf_o), (lwb_o, kb_o, bb_o))):
        a = iclr[:, d * w:(d + 1) * w]
        lw_o[0] = logw[:, d * w:(d + 1) * w]
        k_o[0] = k * (1.0 + (a - 1.0) * ka)
        b_o[0] = kk * a


def _rwkv_prep(p, vfirst, consts, quad, tm):
    b, l, n_cols = p.shape
    w = consts["k_k"].shape[-1]
    has_vres = vfirst is not None
    n_rwkv = consts["n_rwkv"]
    n_vres = consts["n_vres"]
    parts = 4 if quad else 2
    segments = _direction_segments(n_rwkv, parts, 0, w) + _direction_segments(n_vres, parts, n_rwkv, w)
    kern = functools.partial(_rwkv_prep_kernel, quad=quad, has_vres=has_vres, segments=segments,
                             width=w, n_cols=n_cols)
    rows = tm // GRID_W
    n_rows = l // GRID_W
    in_specs = [pl.BlockSpec((1, tm, n_cols), lambda i, j: (i, j, 0))]
    args = [p]
    if quad:
        in_specs += [
            pl.BlockSpec((1, GRID_W, n_cols), lambda i, j: (i, jnp.maximum(j * rows - 1, 0), 0)),
            pl.BlockSpec((1, GRID_W, n_cols), lambda i, j: (i, jnp.minimum((j + 1) * rows, n_rows - 1), 0)),
        ]
        args += [p, p]
    if has_vres:
        in_specs.append(pl.BlockSpec((1, tm, w), lambda i, j: (i, j, 0)))
        args.append(vfirst)
    for name in ("mu", "w0", "w_up", "a0", "a_up", "g_up", "v_up", "v0", "k_k", "k_a", "r_k", "ones"):
        arr = consts[name]
        in_specs.append(pl.BlockSpec(arr.shape, lambda i, j: (0, 0)))
        args.append(arr)
    return pl.pallas_call(
        kern,
        grid=(b, l // tm),
        in_specs=in_specs,
        out_specs=[pl.BlockSpec((1, tm, w), lambda i, j: (i, j, 0))] * 11,
        out_shape=[jax.ShapeDtypeStruct((b, l, w), F32)] * 11,
        compiler_params=_params("arbitrary", "arbitrary"),
        name="rwkv_prep",
    )(*args)


def _block_diag(y):
    lane = lax.broadcasted_iota(jnp.int32, y.shape, 1)
    return jnp.concatenate([jnp.where(lane < HEAD_DIM, y, 0.0), jnp.where(lane >= HEAD_DIM, y, 0.0)], axis=0)


def _head_diag_mask():
    r = lax.broadcasted_iota(jnp.int32, (PAIR, PAIR), 0)
    c = lax.broadcasted_iota(jnp.int32, (PAIR, PAIR), 1)
    return (r < HEAD_DIM) == (c < HEAD_DIM)


def _rwkv_stream(r, v, kk, lw, kd, bd, reverse):
    c = r.shape[0]
    t_idx = lax.broadcasted_iota(jnp.int32, (c, c), 0)
    s_idx = lax.broadcasted_iota(jnp.int32, (c, c), 1)
    tri = jnp.where((s_idx >= t_idx) if reverse else (s_idx <= t_idx), 1.0, 0.0).astype(BF16)
    lc = _mm_exact_lhs(tri, lw)
    l_end = lc[0:1] if reverse else lc[c - 1:c]
    e_inv = jnp.exp(-lc)
    to_end = jnp.exp(l_end - lc)
    row = lax.broadcasted_iota(jnp.int32, (c, PAIR), 0)
    col = lax.broadcasted_iota(jnp.int32, (c, PAIR), 1) % HEAD_DIM
    return dict(
        v=v, r_t=r * jnp.exp(lc), a_t=-kk * jnp.exp(lc - lw), k_t=kd * e_inv, b_t=bd * e_inv,
        k_hat=kd * to_end, b_hat=bd * to_end, p_end=jnp.exp(l_end),
        before=(col > row) if reverse else (col < row),
        upto=(col >= row) if reverse else (col <= row),
        eye=jnp.where(col == row, 1.0, 0.0))


def _rwkv_step(streams, s_scr, y_refs, passes):
    c = streams[0]["v"].shape[0]
    n_pairs = streams[0]["v"].shape[1] // PAIR
    chains = [(d, p) for d in range(len(streams)) for p in range(n_pairs)]
    sl = lambda p: slice(p * PAIR, (p + 1) * PAIR)
    diag = _head_diag_mask()

    z = {}
    zs = {}
    s0 = {}
    for ch in chains:
        d, p = ch
        st = streams[d]
        z[ch] = jnp.concatenate([st["a_t"][:, sl(p)], st["r_t"][:, sl(p)]], axis=0)
        s0[ch] = s_scr[d, p]
        zs[ch] = _mm(z[ch], s0[ch], NT, passes)
    l_ab, m_rb, l_ak, m_rk = {}, {}, {}, {}
    for ch in chains:
        d, p = ch
        st = streams[d]
        gb = _mm(z[ch], _block_diag(st["b_t"][:, sl(p)]), NT, passes)
        gk = _mm(z[ch], _block_diag(st["k_t"][:, sl(p)]), NT, passes)
        l_ab[ch] = jnp.where(st["before"], gb[:c], 0.0)
        m_rb[ch] = jnp.where(st["upto"], gb[c:], 0.0)
        l_ak[ch] = jnp.where(st["before"], gk[:c], 0.0)
        m_rk[ch] = jnp.where(st["upto"], gk[c:], 0.0)

    inv = {ch: streams[ch[0]]["eye"] + l_ab[ch] for ch in chains}
    power = {ch: _mm(l_ab[ch], _block_diag(l_ab[ch]), NN, passes) for ch in chains}
    kv = {ch: _mm(jnp.concatenate([l_ak[ch], m_rk[ch]], axis=0),
                  _block_diag(streams[ch[0]]["v"][:, sl(ch[1])]), NN, passes) for ch in chains}
    for _ in range(int(np.log2(c)) - 2):
        for ch in chains:
            prod = _mm(jnp.concatenate([power[ch], inv[ch]], axis=0), _block_diag(power[ch]), NN, passes)
            inv[ch] = inv[ch] + prod[c:]
            power[ch] = prod[:c]
    for ch in chains:
        inv[ch] = inv[ch] + _mm(inv[ch], _block_diag(power[ch]), NN, passes)

    u = {ch: _mm(inv[ch], _block_diag(zs[ch][:c] + kv[ch][:c]), NN, passes) for ch in chains}
    for ch in chains:
        d, p = ch
        y_refs[d][0, :, sl(p)] = zs[ch][c:] + kv[ch][c:] + _mm(m_rb[ch], _block_diag(u[ch]), NN, passes)
    for ch in chains:
        d, p = ch
        st = streams[d]
        uv_t = jnp.concatenate([u[ch], st["v"][:, sl(p)]], axis=0).T
        upd = _mm(uv_t, jnp.concatenate([st["b_hat"][:, sl(p)], st["k_hat"][:, sl(p)]], axis=0), NN, passes)
        s_scr[d, p] = s0[ch] * st["p_end"][:, sl(p)] + jnp.where(diag, upd, 0.0)


def _rwkv_scan_kernel(rf, vf, kkf, lwf, kf, bf, rb, vb, kkb, lwb, kb, bb, s0_ref,
                      yf_ref, yb_ref, sout_ref, s_scr, *, passes):
    i = pl.program_id(1)

    @pl.when(i == 0)
    def _():
        s_scr[...] = s0_ref[:, 0]

    streams = [_rwkv_stream(rf[0], vf[0], kkf[0], lwf[0], kf[0], bf[0], False),
               _rwkv_stream(rb[0], vb[0], kkb[0], lwb[0], kb[0], bb[0], True)]
    _rwkv_step(streams, s_scr, (yf_ref, yb_ref), passes)

    @pl.when(i == pl.num_programs(1) - 1)
    def _():
        sout_ref[:, 0] = s_scr[...]


def _rwkv_scan(r, v, kk, lwf, lwb, kf, kb, bf, bb, s0, passes=3):
    b, l, w = r.shape
    c = RWKV_CHUNK
    nc = l // c
    n_pairs = w // PAIR
    fwd = pl.BlockSpec((1, c, w), lambda i, j: (i, j, 0))
    bwd = pl.BlockSpec((1, c, w), lambda i, j: (i, nc - 1 - j, 0))
    st = pl.BlockSpec((2, 1, n_pairs, PAIR, PAIR), lambda i, j: (0, i, 0, 0, 0))
    return pl.pallas_call(
        functools.partial(_rwkv_scan_kernel, passes=passes),
        grid=(b, nc),
        in_specs=[fwd] * 6 + [bwd] * 6 + [st],
        out_specs=[fwd, bwd, st],
        out_shape=[jax.ShapeDtypeStruct((b, l, w), F32)] * 2 + [jax.ShapeDtypeStruct(s0.shape, F32)],
        scratch_shapes=[pltpu.VMEM((2, n_pairs, PAIR, PAIR), F32)],
        compiler_params=_params("arbitrary", "arbitrary"),
        name="rwkv_scan",
    )(r, v, kk, lwf, kf, bf, r, v, kk, lwb, kb, bb, s0)


def _rope(x, cos, sin_signed):
    w = x.shape[1]
    half = HEAD_DIM // 2
    lane = lax.broadcasted_iota(jnp.int32, x.shape, 1) % HEAD_DIM
    rot = jnp.where(lane < half, pltpu.roll(x, w - half, 1), pltpu.roll(x, half, 1))
    return x * cos + rot * sin_signed


def _ret_chunk(q, k, v, inner_ref, qdec, kdec, cdec, s_ref, o_ref, scale):
    n_pairs = q.shape[1] // PAIR
    kq = k * scale
    diag = _head_diag_mask()
    for p in range(n_pairs):
        sl = slice(p * PAIR, (p + 1) * PAIR)
        q_p, k_p, v_p = q[:, sl], kq[:, sl], v[:, sl]
        scores = _mm(q_p, _block_diag(k_p), NT) * inner_ref[p]
        s0 = s_ref[p]
        inner = _mm(scores, _block_diag(v_p))
        cross = _mm(q_p, s0) * qdec[:, sl]
        o_ref[0, :, sl] = inner + cross
        upd = _mm((k_p * kdec[:, sl]).T, v_p)
        s_ref[p] = s0 * cdec[:, sl] + jnp.where(diag, upd, 0.0)


def _ret_scan_kernel(qkvf, cosf, sinf, qkvb, cosb, sinb, inf_ref, inb_ref, dec_ref, s0_ref,
                     of_ref, ob_ref, sout_ref, s_scr, *, width):
    i = pl.program_id(1)

    @pl.when(i == 0)
    def _():
        s_scr[...] = s0_ref[:, 0]

    scale = HEAD_DIM ** -0.5
    w = width
    for d, (qkv, cos, sin, inner_ref, o_ref) in enumerate(
            ((qkvf, cosf, sinf, inf_ref, of_ref), (qkvb, cosb, sinb, inb_ref, ob_ref))):
        x = qkv[0]
        q = _rope(x[:, :w], cos[...], sin[...])
        k = _rope(x[:, w:2 * w], cos[...], sin[...])
        _ret_chunk(q, k, x[:, 2 * w:], inner_ref, dec_ref[d, 0], dec_ref[d, 1], dec_ref[d, 2, 0:1],
                   s_scr.at[d], o_ref, scale)

    @pl.when(i == pl.num_programs(1) - 1)
    def _():
        sout_ref[:, 0] = s_scr[...]


def _ret_scan(qkv, cos, sin, tables, s0):
    b, l, w3 = qkv.shape
    w = w3 // 3
    c = RET_CHUNK
    nc = l // c
    n_pairs = w // PAIR
    inner_f, inner_b, dec = tables
    st = pl.BlockSpec((2, 1, n_pairs, PAIR, PAIR), lambda i, j: (0, i, 0, 0, 0))
    const3 = pl.BlockSpec(inner_f.shape, lambda i, j: (0, 0, 0))
    return pl.pallas_call(
        functools.partial(_ret_scan_kernel, width=w),
        grid=(b, nc),
        in_specs=[pl.BlockSpec((1, c, w3), lambda i, j: (i, j, 0)),
                  pl.BlockSpec((c, w), lambda i, j: (j, 0)),
                  pl.BlockSpec((c, w), lambda i, j: (j, 0)),
                  pl.BlockSpec((1, c, w3), lambda i, j: (i, nc - 1 - j, 0)),
                  pl.BlockSpec((c, w), lambda i, j: (nc - 1 - j, 0)),
                  pl.BlockSpec((c, w), lambda i, j: (nc - 1 - j, 0)),
                  const3, const3,
                  pl.BlockSpec(dec.shape, lambda i, j: (0, 0, 0, 0)),
                  st],
        out_specs=[pl.BlockSpec((1, c, w), lambda i, j: (i, j, 0)),
                   pl.BlockSpec((1, c, w), lambda i, j: (i, nc - 1 - j, 0)),
                   st],
        out_shape=[jax.ShapeDtypeStruct((b, l, w), F32)] * 2 + [jax.ShapeDtypeStruct(s0.shape, F32)],
        scratch_shapes=[pltpu.VMEM((2, n_pairs, PAIR, PAIR), F32)],
        compiler_params=_params("arbitrary", "arbitrary"),
        name="ret_scan",
    )(qkv, cos, sin, qkv, cos, sin, inner_f, inner_b, dec, s0)


def _ret_tables(n_heads):
    c = RET_CHUNK
    lg = np.log(1.0 - 2.0 ** (-5.0 - np.arange(n_heads, dtype=np.float64)))
    idx = np.arange(c, dtype=np.float64)
    diff = idx[:, None] - idx[None, :]

    def inner(log_gamma, reverse):
        dd = -diff if reverse else diff
        m = np.where(dd >= 0, np.exp(log_gamma[:, None, None] * np.maximum(dd, 0.0)), 0.0)
        return m.reshape(n_heads // 2, 2, c, c).transpose(0, 2, 1, 3).reshape(n_heads // 2, c, 2 * c)

    def per_lane(tab):
        return np.repeat(tab, HEAD_DIM, axis=1)

    dec = []
    for reverse in (False, True):
        g = lg[::-1] if reverse else lg
        pos = (c - 1.0 - idx) if reverse else idx
        qd = per_lane(np.exp(g[None, :] * (pos[:, None] + 1.0)))
        kd = per_lane(np.exp(g[None, :] * (c - 1.0 - pos[:, None])))
        cd = per_lane(np.broadcast_to(np.exp(g * c)[None, :], (c, n_heads)))
        dec.append(np.stack([qd, kd, cd]))
    return (jnp.asarray(inner(lg, False), F32), jnp.asarray(inner(lg[::-1], True), F32),
            jnp.asarray(np.stack(dec), F32))


def _rope_tables(l, n_heads, rotate):
    if not rotate:
        return jnp.ones((l, n_heads * HEAD_DIM), F32), jnp.zeros((l, n_heads * HEAD_DIM), F32)
    rows = l // GRID_W
    row = jnp.repeat(jnp.arange(rows, dtype=F32), GRID_W)
    col = jnp.tile(jnp.arange(GRID_W, dtype=F32), rows)
    nf = HEAD_DIM // 4
    freqs = ROPE_BASE ** (-jnp.arange(nf, dtype=F32) / nf)
    ang = jnp.concatenate([row[:, None] * freqs, col[:, None] * freqs], axis=-1)
    cos = jnp.tile(jnp.cos(ang), (1, 2 * n_heads))
    sin = jnp.tile(jnp.concatenate([-jnp.sin(ang), jnp.sin(ang)], axis=-1), (1, n_heads))
    return cos, sin


def _group_norm(y, ones, w, b, eps):
    inv_n = 1.0 / HEAD_DIM
    mu = _dot(y.astype(BF16), ones) * inv_n
    d = y - mu
    var = _dot((d * d).astype(BF16), ones) * inv_n
    return d * lax.rsqrt(var + eps) * w + b


def _out_ffn_kernel(x_ref, yf_ref, yb_ref, bonus_ref, gate_ref, of_ref, ob_ref, pg_ref,
                    lnw_ref, lnb_ref, gnw_ref, gnb_ref, ones_ref, wo_ref,
                    g1_ref, sh2_ref, sc2_ref, g2_ref, n2_ref, nf_ref, wfi_ref, wfo_ref,
                    o_ref, *, width, hidden, th, final_norm):
    ones = ones_ref[...]
    w = width
    o_rwkv = (_group_norm(yf_ref[0] + yb_ref[0], ones, lnw_ref[...], lnb_ref[...], RWKV_GN_EPS)
              + bonus_ref[0]) * gate_ref[0]
    pg = pg_ref[0]
    o_ret = (_group_norm(of_ref[0], ones, gnw_ref[...], gnb_ref[...], RET_GN_EPS) * _silu(pg[:, :w])
             + _group_norm(ob_ref[0], ones, gnw_ref[...], gnb_ref[...], RET_GN_EPS) * _silu(pg[:, w:]))
    attn = _dot(o_rwkv.astype(BF16), wo_ref[:w]) + _dot(o_ret.astype(BF16), wo_ref[w:])
    x1 = x_ref[0] + g1_ref[0] * attn
    h = _modnorm(x1, n2_ref[...], sh2_ref[0], sc2_ref[0]).astype(BF16)
    acc = None
    for c0 in range(0, hidden, th):
        gate = _dot(h, wfi_ref[:, c0:c0 + th])
        up = _dot(h, wfi_ref[:, hidden + c0:hidden + c0 + th])
        part = _dot((_silu(gate) * up).astype(BF16), wfo_ref[c0:c0 + th])
        acc = part if acc is None else acc + part
    x2 = x1 + g2_ref[0] * acc
    if final_norm:
        ms = jnp.mean(x2 * x2, axis=-1, keepdims=True)
        x2 = (x2 * lax.rsqrt(ms + NORM_EPS)) * nf_ref[...]
    o_ref[0] = x2


def _out_ffn(x, yf, yb, bonus, gate, of, ob, pg, consts, mods, final_norm, tm):
    b, l, d = x.shape
    w = yf.shape[-1]
    hidden = consts["w_ffn_out"].shape[0]
    th = hidden // 2 if (hidden // 2) % 128 == 0 else hidden
    bm = mods[0].shape[0]
    mod_map = (lambda i, j: (i, 0, 0)) if bm > 1 else (lambda i, j: (0, 0, 0))
    tok = lambda width: pl.BlockSpec((1, tm, width), lambda i, j: (i, j, 0))
    const = lambda arr: pl.BlockSpec(arr.shape, lambda i, j: (0,) * arr.ndim, pipeline_mode=pl.Buffered(1))
    names = ("ln_w", "ln_b", "gn_w", "gn_b", "ones", "w_out")
    tail = ("norm2", "norm_f", "w_ffn_in", "w_ffn_out")
    in_specs = ([tok(d)] + [tok(w)] * 6 + [tok(2 * w)] + [const(consts[n]) for n in names]
                + [pl.BlockSpec((1, 1, d), mod_map)] * 4 + [const(consts[n]) for n in tail])
    args = ([x, yf, yb, bonus, gate, of, ob, pg] + [consts[n] for n in names] + list(mods)
            + [consts[n] for n in tail])
    kern = functools.partial(_out_ffn_kernel, width=w, hidden=hidden, th=th, final_norm=final_norm)
    return pl.pallas_call(
        kern,
        grid=(b, l // tm),
        in_specs=in_specs,
        out_specs=tok(d),
        out_shape=jax.ShapeDtypeStruct((b, l, d), F32),
        compiler_params=_params("arbitrary", "arbitrary"),
        name="out_ffn",
    )(*args)


def _pad_rows(a, rows, offset=0):
    out = jnp.zeros((rows,) + a.shape[1:], a.dtype)
    return out.at[offset:offset + a.shape[0]].set(a)


def _layer_consts(i, w_in, w_vres_down, mu_rwkv, mu_vres, w0, w_up, a0, a_up, g_up, k_k, k_a, r_k, v0, v_up,
                  ln_x_w, ln_x_b, ret_gn_w, ret_gn_b, w_out, w_ffn_in, w_ffn_out, norm2, norm_f):
    width = w0.shape[-1]
    n_decay, n_iclr, n_gate = w_up.shape[2], a_up.shape[2], g_up.shape[1]
    n_vres = v_up.shape[1]
    n_rwkv = 3 * width + n_decay + n_iclr + n_gate
    assert n_decay + n_iclr == 128 and n_gate + n_vres <= 256
    rwkv_cols = 3 * width + 128 + 256
    d = w_in.shape[1]
    has_vres = i > 0
    wi = w_in[i]
    pad = jnp.zeros((d, rwkv_cols - n_rwkv - n_vres), F32)
    vres_cols = w_vres_down[i - 1] if has_vres else jnp.zeros((d, n_vres), F32)
    w_proj = jnp.concatenate([wi[:, :n_rwkv], vres_cols, pad, wi[:, n_rwkv:]], axis=1).astype(BF16)
    mu_v = mu_vres[i - 1] if has_vres else jnp.zeros((n_vres,), F32)
    mu = jnp.concatenate([mu_rwkv[i], mu_v, jnp.zeros((rwkv_cols - n_rwkv - n_vres,), F32)])[None]
    cat2 = lambda a: jnp.concatenate([a[0], a[1]], axis=-1)
    head = jnp.arange(width) // HEAD_DIM
    ones = (head[:, None] == head[None, :]).astype(BF16)
    row = lambda a: a.reshape(1, -1)
    return dict(
        w_proj=w_proj, widths=(rwkv_cols, 3 * width, 2 * width), n_rwkv=n_rwkv, n_vres=n_vres, mu=mu,
        w0=row(cat2(w0[i])), w_up=_pad_rows(cat2(w_up[i]), 128), a0=row(cat2(a0[i])),
        a_up=_pad_rows(cat2(a_up[i]), 128, n_decay), g_up=_pad_rows(g_up[i], 256),
        v_up=_pad_rows(v_up[i - 1] if has_vres else jnp.zeros((n_vres, width), F32), 256, n_gate),
        v0=row(v0[i - 1] if has_vres else jnp.zeros((width,), F32)),
        k_k=row(k_k[i]), k_a=row(k_a[i]), r_k=row(r_k[i]), ones=ones,
        ln_w=row(ln_x_w[i]), ln_b=row(ln_x_b[i]), gn_w=row(ret_gn_w[i]), gn_b=row(ret_gn_b[i]),
        w_out=w_out[i].astype(BF16), w_ffn_in=w_ffn_in[i].astype(BF16), w_ffn_out=w_ffn_out[i].astype(BF16),
        norm2=row(norm2[i]), norm_f=row(norm_f),
    )


def _mixers(x, norm1, shift, scale, consts, vfirst, s_rwkv, s_ret, rope, tables, quad, tm):
    p_rwkv, p_qkv, p_g = _in_proj(x, norm1, shift, scale, consts["w_proj"], consts["widths"], tm)
    r, v, kk, bonus, gate, lwf, lwb, kf, kb, bf, bb = _rwkv_prep(
        p_rwkv, vfirst, consts, quad, min(TM_PREP, x.shape[1]) if quad else x.shape[1])
    yf, yb, s_rwkv = _rwkv_scan(r, v, kk, lwf, lwb, kf, kb, bf, bb, s_rwkv)
    of, ob, s_ret = _ret_scan(p_qkv, rope[0], rope[1], tables, s_ret)
    return (yf, yb, bonus, gate, of, ob, p_g), v, s_rwkv, s_ret


def kernel(x, c, ctx, c_ctx, w_ada, b_ada, norm1, norm2, norm_f, w_in, w_vres_down, mu_rwkv, mu_vres, w0, w_up, a0, a_up, g_up, k_k, k_a, r_k, v0, v_up, ln_x_w, ln_x_b, ret_gn_w, ret_gn_b, w_out, w_ffn_in, w_ffn_out):
    b, l, d = x.shape
    l_ctx = ctx.shape[1]
    depth = w_in.shape[0]
    width = w0.shape[-1]
    n_heads = width // HEAD_DIM
    n_pairs = n_heads // 2
    tm = min(TM_PROJ, l)
    tm_ctx = min(TM_PROJ, l_ctx)
    tm_ffn = min(TM_FFN, l)
    tm_ffn_ctx = min(TM_FFN, l_ctx)

    m_pad = -(-(b + 1) // 8) * 8
    cond = jnp.zeros((m_pad, d), F32).at[:b].set(c).at[b].set(c_ctx)
    mod = _ada(cond, w_ada, b_ada)

    tables = _ret_tables(n_heads)
    rope_l = _rope_tables(l, n_heads, True)
    rope_c = _rope_tables(l_ctx, n_heads, False)
    zero_state = jnp.zeros((2, b, n_pairs, PAIR, PAIR), F32)

    vfirst_c = vfirst_l = None
    for i in range(depth):
        last = i == depth - 1
        consts = _layer_consts(i, w_in, w_vres_down, mu_rwkv, mu_vres, w0, w_up, a0, a_up, g_up, k_k, k_a, r_k,
                               v0, v_up, ln_x_w, ln_x_b, ret_gn_w, ret_gn_b, w_out, w_ffn_in, w_ffn_out,
                               norm2, norm_f)
        mods_l = [mod[i, :b, None, j * d:(j + 1) * d] for j in range(6)]
        mods_c = [mod[i, b:b + 1, None, j * d:(j + 1) * d] for j in range(6)]
        n1 = norm1[i][None]

        mix_c, v_c, s_rwkv, s_ret = _mixers(ctx, n1, mods_c[0], mods_c[1], consts, vfirst_c,
                                            zero_state, zero_state, rope_c, tables, False, tm_ctx)
        mix_l, v_l, _, _ = _mixers(x, n1, mods_l[0], mods_l[1], consts, vfirst_l,
                                   s_rwkv, s_ret, rope_l, tables, True, tm)
        if i == 0:
            vfirst_c, vfirst_l = v_c, v_l
        x = _out_ffn(x, *mix_l, consts, (mods_l[2], mods_l[3], mods_l[4], mods_l[5]), last, tm_ffn)
        if not last:
            ctx = _out_ffn(ctx, *mix_c, consts, (mods_c[2], mods_c[3], mods_c[4], mods_c[5]), False, tm_ffn_ctx)
    return x
```

```python
import functools

import numpy as np
import jax
import jax.numpy as jnp
from jax import lax
from jax.experimental import pallas as pl
from jax.experimental.pallas import tpu as pltpu

F32 = jnp.float32
BF16 = jnp.bfloat16

GRID_W = 64
HEAD_DIM = 64
PAIR = 2 * HEAD_DIM
RWKV_CHUNK = 64
RET_CHUNK = 128
ROPE_BASE = 10000.0
NORM_EPS = 1e-6
RWKV_GN_EPS = 64e-5
RET_GN_EPS = 1e-5
VMEM_LIMIT = 56 * 1024 * 1024
TM_PROJ = 512
TM_PREP = 256
TM_FFN = 256
SCAN_BATCH = 2
RWKV_PASSES = dict(state=1, gram_b=1, gram_k=1, inverse=1, kv=1, solve=1, out=1, update=1)

NN = ((1,), (0,))
NT = ((1,), (1,))


def _dot(a, b, dims=NN):
    return lax.dot_general(a, b, (dims, ((), ())), preferred_element_type=F32)


def _split(a):
    hi = a.astype(BF16)
    lo = (a - hi.astype(F32)).astype(BF16)
    return hi, lo


def _mm(a, b, dims=NN, passes=1):
    if passes == 1:
        return _dot(a.astype(BF16), b.astype(BF16), dims)
    ah, al = _split(a)
    bh, bl = _split(b)
    lhs = jnp.concatenate([ah, al, ah], axis=1)
    rhs = jnp.concatenate([bh, bh, bl], axis=0 if dims == NN else 1)
    return _dot(lhs, rhs, dims)


def _mm_exact_lhs(a_bf16, b, pieces=3):
    acc = None
    rem = b
    for _ in range(pieces):
        p = rem.astype(BF16)
        t = _dot(a_bf16, p)
        acc = t if acc is None else acc + t
        rem = rem - p.astype(F32)
    return acc


def _mm_exact_rhs(a, b_bf16, pieces=3):
    acc = None
    rem = a
    for _ in range(pieces):
        p = rem.astype(BF16)
        t = _dot(p, b_bf16)
        acc = t if acc is None else acc + t
        rem = rem - p.astype(F32)
    return acc


def _sigmoid(x):
    return 1.0 / (1.0 + jnp.exp(-x))


def _silu(x):
    return x * _sigmoid(x)


def _softplus(z):
    return jnp.maximum(z, 0.0) + jnp.log(1.0 + jnp.exp(-jnp.abs(z)))


def _params(*sem):
    return pltpu.CompilerParams(dimension_semantics=sem, vmem_limit_bytes=VMEM_LIMIT)


def _ada_kernel(c_ref, w_ref, b_ref, o_ref):
    o_ref[0] = _mm(_silu(c_ref[...]), w_ref[0], passes=3) + b_ref[0]


def _ada(cond, w_ada, b_ada):
    depth, d, n = w_ada.shape
    m = cond.shape[0]
    tn = n // 4
    return pl.pallas_call(
        _ada_kernel,
        grid=(depth, n // tn),
        in_specs=[pl.BlockSpec((m, d), lambda i, j: (0, 0)),
                  pl.BlockSpec((1, d, tn), lambda i, j: (i, 0, j)),
                  pl.BlockSpec((1, 1, tn), lambda i, j: (i, 0, j))],
        out_specs=pl.BlockSpec((1, m, tn), lambda i, j: (i, 0, j)),
        out_shape=jax.ShapeDtypeStruct((depth, m, n), F32),
        compiler_params=_params("arbitrary", "arbitrary"),
        name="ada",
    )(cond, w_ada, b_ada.reshape(depth, 1, n))


def _modnorm(x, g, shift, scale):
    ms = jnp.mean(x * x, axis=-1, keepdims=True)
    y = (x * lax.rsqrt(ms + NORM_EPS)) * g
    return y * (1.0 + scale) + shift


def _in_proj_kernel(x_ref, g_ref, sh_ref, sc_ref, w_ref, *out_refs, widths, tn):
    h = _modnorm(x_ref[0], g_ref[...], sh_ref[0], sc_ref[0]).astype(BF16)
    col = 0
    for o_ref, width in zip(out_refs, widths):
        for c0 in range(0, width, tn):
            c1 = min(c0 + tn, width)
            o_ref[0, :, c0:c1] = _dot(h, w_ref[:, col + c0:col + c1])
        col += width


def _in_proj(x, g, shift, scale, w, widths, tm):
    b, l, d = x.shape
    bm = shift.shape[0]
    mod_map = (lambda i, j: (i, 0, 0)) if bm > 1 else (lambda i, j: (0, 0, 0))
    kern = functools.partial(_in_proj_kernel, widths=widths, tn=512)
    return pl.pallas_call(
        kern,
        grid=(b, l // tm),
        in_specs=[pl.BlockSpec((1, tm, d), lambda i, j: (i, j, 0)),
                  pl.BlockSpec((1, d), lambda i, j: (0, 0)),
                  pl.BlockSpec((1, 1, d), mod_map),
                  pl.BlockSpec((1, 1, d), mod_map),
                  pl.BlockSpec(w.shape, lambda i, j: (0, 0))],
        out_specs=[pl.BlockSpec((1, tm, wd), lambda i, j: (i, j, 0)) for wd in widths],
        out_shape=[jax.ShapeDtypeStruct((b, l, wd), F32) for wd in widths],
        compiler_params=_params("arbitrary", "arbitrary"),
        name="in_proj",
    )(x, g, shift, scale, w)


def _direction_segments(n_cols, parts, offset, width):
    q = n_cols // parts
    return [(offset + i * q, offset + (i + 1) * q, i) for i in range(parts)]


def _shift_lerp(x, mu_ref, shifted_fn, segments, n_cols):
    outs = []
    for g0 in range(0, n_cols, 128):
        xg = x[:, g0:g0 + 128]
        lane = lax.broadcasted_iota(jnp.int32, xg.shape, 1) + g0
        s = None
        for (c0, c1, d) in segments:
            lo, hi = max(c0, g0), min(c1, g0 + 128)
            if lo >= hi:
                continue
            sd = shifted_fn(xg, d, g0)
            if lo == g0 and hi == g0 + 128:
                s = sd
            else:
                part = jnp.where((lane >= lo) & (lane < hi), sd, 0.0)
                s = part if s is None else s + part
        if s is None:
            s = jnp.zeros_like(xg)
        outs.append(xg + (s - xg) * mu_ref[:, g0:g0 + 128])
    return jnp.concatenate(outs, axis=1)


def _rwkv_prep_kernel(*refs, quad, has_vres, segments, width, n_cols):
    it = iter(refs)
    cur_ref = next(it)
    prev_ref = next(it) if quad else None
    next_ref = next(it) if quad else None
    vfirst_ref = next(it) if has_vres else None
    (mu_ref, w0_ref, wup_ref, a0_ref, aup_ref, gup_ref, vup_ref, v0_ref, kk_ref, ka_ref, rk_ref,
     ones_ref) = (next(it) for _ in range(12))
    (r_o, v_o, kk_o, bonus_o, gate_o, lwf_o, lwb_o, kf_o, kb_o, bf_o, bb_o) = (next(it) for _ in range(11))

    x = cur_ref[0]
    t = x.shape[0]
    row = lax.broadcasted_iota(jnp.int32, (t, 128), 0)
    if quad:
        gcol = row % GRID_W
        first = pl.program_id(1) == 0
        last = pl.program_id(1) == pl.num_programs(1) - 1

        def shifted(xg, d, g0):
            if d == 0:
                return jnp.where(gcol == 0, 0.0, pltpu.roll(xg, 1, 0))
            if d == 1:
                return jnp.where(gcol == GRID_W - 1, 0.0, pltpu.roll(xg, t - 1, 0))
            if d == 2:
                halo = jnp.where(first, 0.0, prev_ref[0, :, g0:g0 + 128])
                return jnp.concatenate([halo, xg[:t - GRID_W]], axis=0)
            halo = jnp.where(last, 0.0, next_ref[0, :, g0:g0 + 128])
            return jnp.concatenate([xg[GRID_W:], halo], axis=0)
    else:
        def shifted(xg, d, g0):
            if d == 0:
                return jnp.where(row == 0, 0.0, pltpu.roll(xg, 1, 0))
            return jnp.where(row == t - 1, 0.0, pltpu.roll(xg, t - 1, 0))

    u = _shift_lerp(x, mu_ref, shifted, segments, n_cols)
    w = width
    r, k, v = u[:, :w], u[:, w:2 * w], u[:, 2 * w:3 * w]
    lo_wa = u[:, 3 * w:3 * w + 128]
    lo_gv = u[:, 3 * w + 128:3 * w + 384]

    logw = -jnp.exp(-_softplus(-(w0_ref[...] + _mm(jnp.tanh(lo_wa), wup_ref[...], passes=3))) - 0.5)
    iclr = _sigmoid(a0_ref[...] + _mm(lo_wa, aup_ref[...], passes=3))
    gate = _mm(_sigmoid(lo_gv), gup_ref[...], passes=3)
    if has_vres:
        mix = _sigmoid(v0_ref[...] + _mm(lo_gv, vup_ref[...], passes=3))
        v = v + (vfirst_ref[0] - v) * mix

    ones = ones_ref[...]
    kk = k * kk_ref[...]
    kk = kk * lax.rsqrt(jnp.maximum(_mm_exact_rhs(kk * kk, ones), 1e-24))
    bonus = _mm_exact_rhs(r * k * rk_ref[...], ones) * v

    r_o[0], v_o[0], kk_o[0], bonus_o[0], gate_o[0] = r, v, kk, bonus, gate
    ka = ka_ref[...]
    for d, (lw_o, k_o, b_o) in enumerate(((lwf_o, kf_o, bf_o), (lwb_o, kb_o, bb_o))):
        a = iclr[:, d * w:(d + 1) * w]
        lw_o[0] = logw[:, d * w:(d + 1) * w]
        k_o[0] = k * (1.0 + (a - 1.0) * ka)
        b_o[0] = kk * a


def _rwkv_prep(p, vfirst, consts, quad, tm):
    b, l, n_cols = p.shape
    w = consts["k_k"].shape[-1]
    has_vres = vfirst is not None
    n_rwkv = consts["n_rwkv"]
    n_vres = consts["n_vres"]
    parts = 4 if quad else 2
    segments = _direction_segments(n_rwkv, parts, 0, w) + _direction_segments(n_vres, parts, n_rwkv, w)
    kern = functools.partial(_rwkv_prep_kernel, quad=quad, has_vres=has_vres, segments=segments,
                             width=w, n_cols=n_cols)
    rows = tm // GRID_W
    n_rows = l // GRID_W
    in_specs = [pl.BlockSpec((1, tm, n_cols), lambda i, j: (i, j, 0))]
    args = [p]
    if quad:
        in_specs += [
            pl.BlockSpec((1, GRID_W, n_cols), lambda i, j: (i, jnp.maximum(j * rows - 1, 0), 0)),
            pl.BlockSpec((1, GRID_W, n_cols), lambda i, j: (i, jnp.minimum((j + 1) * rows, n_rows - 1), 0)),
        ]
        args += [p, p]
    if has_vres:
        in_specs.append(pl.BlockSpec((1, tm, w), lambda i, j: (i, j, 0)))
        args.append(vfirst)
    for name in ("mu", "w0", "w_up", "a0", "a_up", "g_up", "v_up", "v0", "k_k", "k_a", "r_k", "ones"):
        arr = consts[name]
        in_specs.append(pl.BlockSpec(arr.shape, lambda i, j: (0, 0)))
        args.append(arr)
    return pl.pallas_call(
        kern,
        grid=(b, l // tm),
        in_specs=in_specs,
        out_specs=[pl.BlockSpec((1, tm, w), lambda i, j: (i, j, 0))] * 11,
        out_shape=[jax.ShapeDtypeStruct((b, l, w), F32)] * 11,
        compiler_params=_params("arbitrary", "arbitrary"),
        name="rwkv_prep",
    )(*args)


def _block_diag(y):
    lane = lax.broadcasted_iota(jnp.int32, y.shape, 1)
    return jnp.concatenate([jnp.where(lane < HEAD_DIM, y, 0.0), jnp.where(lane >= HEAD_DIM, y, 0.0)], axis=0)


def _head_diag_mask():
    r = lax.broadcasted_iota(jnp.int32, (PAIR, PAIR), 0)
    c = lax.broadcasted_iota(jnp.int32, (PAIR, PAIR), 1)
    return (r < HEAD_DIM) == (c < HEAD_DIM)


def _rwkv_stream(r, v, kk, lw, kd, bd, reverse, state, out):
    c = r.shape[0]
    t_idx = lax.broadcasted_iota(jnp.int32, (c, c), 0)
    s_idx = lax.broadcasted_iota(jnp.int32, (c, c), 1)
    tri = jnp.where((s_idx >= t_idx) if reverse else (s_idx <= t_idx), 1.0, 0.0).astype(BF16)
    lc = _mm_exact_lhs(tri, lw)
    l_end = lc[0:1] if reverse else lc[c - 1:c]
    e_inv = jnp.exp(-lc)
    to_end = jnp.exp(l_end - lc)
    row = lax.broadcasted_iota(jnp.int32, (c, PAIR), 0)
    col = lax.broadcasted_iota(jnp.int32, (c, PAIR), 1) % HEAD_DIM
    return dict(
        state=state, out=out,
        v=v, r_t=r * jnp.exp(lc), a_t=-kk * jnp.exp(lc - lw), k_t=kd * e_inv, b_t=bd * e_inv,
        k_hat=kd * to_end, b_hat=bd * to_end, p_end=jnp.exp(l_end),
        before=(col > row) if reverse else (col < row),
        upto=(col >= row) if reverse else (col <= row),
        eye=jnp.where(col == row, 1.0, 0.0))


def _rwkv_step(streams, s_scr, passes):
    c = streams[0]["v"].shape[0]
    n_pairs = streams[0]["v"].shape[1] // PAIR
    chains = [(k, p) for k in range(len(streams)) for p in range(n_pairs)]
    sl = lambda p: slice(p * PAIR, (p + 1) * PAIR)
    diag = _head_diag_mask()

    z, zs, s0 = {}, {}, {}
    for ch in chains:
        st, p = streams[ch[0]], ch[1]
        z[ch] = jnp.concatenate([st["a_t"][:, sl(p)], st["r_t"][:, sl(p)]], axis=0)
        s0[ch] = s_scr[st["state"] + (p,)]
        zs[ch] = _mm(z[ch], s0[ch], NT, passes["state"])
    l_ab, m_rb, l_ak, m_rk = {}, {}, {}, {}
    for ch in chains:
        st, p = streams[ch[0]], ch[1]
        gb = _mm(z[ch], _block_diag(st["b_t"][:, sl(p)]), NT, passes["gram_b"])
        gk = _mm(z[ch], _block_diag(st["k_t"][:, sl(p)]), NT, passes["gram_k"])
        l_ab[ch] = jnp.where(st["before"], gb[:c], 0.0)
        m_rb[ch] = jnp.where(st["upto"], gb[c:], 0.0)
        l_ak[ch] = jnp.where(st["before"], gk[:c], 0.0)
        m_rk[ch] = jnp.where(st["upto"], gk[c:], 0.0)

    inv = {ch: streams[ch[0]]["eye"] + l_ab[ch] for ch in chains}
    power = {ch: _mm(l_ab[ch], _block_diag(l_ab[ch]), NN, passes["inverse"]) for ch in chains}
    kv = {ch: _mm(jnp.concatenate([l_ak[ch], m_rk[ch]], axis=0),
                  _block_diag(streams[ch[0]]["v"][:, sl(ch[1])]), NN, passes["kv"]) for ch in chains}
    for _ in range(int(np.log2(c)) - 2):
        for ch in chains:
            prod = _mm(jnp.concatenate([power[ch], inv[ch]], axis=0), _block_diag(power[ch]), NN,
                       passes["inverse"])
            inv[ch] = inv[ch] + prod[c:]
            power[ch] = prod[:c]
    for ch in chains:
        inv[ch] = inv[ch] + _mm(inv[ch], _block_diag(power[ch]), NN, passes["inverse"])

    u = {ch: _mm(inv[ch], _block_diag(zs[ch][:c] + kv[ch][:c]), NN, passes["solve"]) for ch in chains}
    for ch in chains:
        st, p = streams[ch[0]], ch[1]
        y_ref, g = st["out"]
        y_ref[g, :, sl(p)] = zs[ch][c:] + kv[ch][c:] + _mm(m_rb[ch], _block_diag(u[ch]), NN, passes["out"])
    for ch in chains:
        st, p = streams[ch[0]], ch[1]
        uv_t =jnp.concatenate([u[ch], st["v"][:, sl(p)]], axis=0).T
        upd = _mm(uv_t, jnp.concatenate([st["b_hat"][:, sl(p)], st["k_hat"][:, sl(p)]], axis=0), NN,
                  passes["update"])
        s_scr[st["state"] + (p,)] = s0[ch] * st["p_end"][:, sl(p)] + jnp.where(diag, upd, 0.0)


def _rwkv_scan_kernel(rf, vf, kkf, lwf, kf, bf, rb, vb, kkb, lwb, kb, bb, s0_ref,
                      yf_ref, yb_ref, sout_ref, s_scr, *, passes):
    i = pl.program_id(1)

    @pl.when(i == 0)
    def _():
        s_scr[...] = s0_ref[...]

    streams = []
    for g in range(rf.shape[0]):
        streams.append(_rwkv_stream(rf[g], vf[g], kkf[g], lwf[g], kf[g], bf[g], False, (0, g), (yf_ref, g)))
        streams.append(_rwkv_stream(rb[g], vb[g], kkb[g], lwb[g], kb[g], bb[g], True, (1, g), (yb_ref, g)))
    _rwkv_step(streams, s_scr, passes)

    @pl.when(i == pl.num_programs(1) - 1)
    def _():
        sout_ref[...] = s_scr[...]


def _rwkv_scan(r, v, kk, lwf, lwb, kf, kb, bf, bb, s0):
    b, l, w = r.shape
    c = RWKV_CHUNK
    nc = l // c
    n_pairs = w // PAIR
    g = SCAN_BATCH if b % SCAN_BATCH == 0 else 1
    fwd = pl.BlockSpec((g, c, w), lambda i, j: (i, j, 0))
    bwd = pl.BlockSpec((g, c, w), lambda i, j: (i, nc - 1 - j, 0))
    st = pl.BlockSpec((2, g, n_pairs, PAIR, PAIR), lambda i, j: (0, i, 0, 0, 0))
    return pl.pallas_call(
        functools.partial(_rwkv_scan_kernel, passes=dict(RWKV_PASSES)),
        grid=(b // g, nc),
        in_specs=[fwd] * 6 + [bwd] * 6 + [st],
        out_specs=[fwd, bwd, st],
        out_shape=[jax.ShapeDtypeStruct((b, l, w), F32)] * 2 + [jax.ShapeDtypeStruct(s0.shape, F32)],
        scratch_shapes=[pltpu.VMEM((2, g, n_pairs, PAIR, PAIR), F32)],
        compiler_params=_params("arbitrary", "arbitrary"),
        name="rwkv_scan",
    )(r, v, kk, lwf, kf, bf, r, v, kk, lwb, kb, bb, s0)


def _rope(x, cos, sin_signed):
    w = x.shape[1]
    half = HEAD_DIM // 2
    lane = lax.broadcasted_iota(jnp.int32, x.shape, 1) % HEAD_DIM
    rot = jnp.where(lane < half, pltpu.roll(x, w - half, 1), pltpu.roll(x, half, 1))
    return x * cos + rot * sin_signed


def _ret_step(streams, s_scr):
    n_pairs = streams[0]["v"].shape[1] // PAIR
    chains = [(k, p) for k in range(len(streams)) for p in range(n_pairs)]
    sl = lambda p: slice(p * PAIR, (p + 1) * PAIR)
    diag = _head_diag_mask()
    scores, cross, s0 = {}, {}, {}
    for ch in chains:
        st, p = streams[ch[0]], ch[1]
        s0[ch] = s_scr[st["state"] + (p,)]
        scores[ch] = _mm(st["q"][:, sl(p)], _block_diag(st["k"][:, sl(p)]), NT) * st["inner"][p]
        cross[ch] = _mm(st["q"][:, sl(p)], s0[ch]) * st["qdec"][:, sl(p)]
    for ch in chains:
        st, p = streams[ch[0]], ch[1]
        o_ref, g = st["out"]
        o_ref[g, :, sl(p)] = _mm(scores[ch], _block_diag(st["v"][:, sl(p)])) + cross[ch]
    for ch in chains:
        st, p = streams[ch[0]], ch[1]
        upd = _mm((st["k"][:, sl(p)] * st["kdec"][:, sl(p)]).T, st["v"][:, sl(p)])
        s_scr[st["state"] + (p,)] = s0[ch] * st["cdec"][:, sl(p)] + jnp.where(diag, upd, 0.0)


def _ret_scan_kernel(qkvf, cosf, sinf, qkvb, cosb, sinb, inf_ref, inb_ref, dec_ref, s0_ref,
                     of_ref, ob_ref, sout_ref, s_scr, *, width):
    i = pl.program_id(1)

    @pl.when(i == 0)
    def _():
        s_scr[...] = s0_ref[...]

    scale = HEAD_DIM ** -0.5
    w = width
    streams = []
    for g in range(qkvf.shape[0]):
        for d, (qkv, cos, sin, inner_ref, o_ref) in enumerate(
                ((qkvf, cosf, sinf, inf_ref, of_ref), (qkvb, cosb, sinb, inb_ref, ob_ref))):
            x = qkv[g]
            streams.append(dict(
                q=_rope(x[:, :w], cos[...], sin[...]), k=_rope(x[:, w:2 * w], cos[...], sin[...]) * scale,
                v=x[:, 2 * w:], inner=inner_ref, qdec=dec_ref[d, 0], kdec=dec_ref[d, 1],
                cdec=dec_ref[d, 2, 0:1], state=(d, g), out=(o_ref, g)))
    _ret_step(streams, s_scr)

    @pl.when(i == pl.num_programs(1) - 1)
    def _():
        sout_ref[...] = s_scr[...]


def _ret_scan(qkv, cos, sin, tables, s0):
    b, l, w3 = qkv.shape
    w = w3 // 3
    c = RET_CHUNK
    nc = l // c
    n_pairs = w // PAIR
    g = SCAN_BATCH if b % SCAN_BATCH == 0 else 1
    inner_f, inner_b, dec = tables
    st = pl.BlockSpec((2, g, n_pairs, PAIR, PAIR), lambda i, j: (0, i, 0, 0, 0))
    const3 = pl.BlockSpec(inner_f.shape, lambda i, j: (0, 0, 0))
    return pl.pallas_call(
        functools.partial(_ret_scan_kernel, width=w),
        grid=(b // g, nc),
        in_specs=[pl.BlockSpec((g, c, w3), lambda i, j: (i, j, 0)),
                  pl.BlockSpec((c, w), lambda i, j: (j, 0)),
                  pl.BlockSpec((c, w), lambda i, j: (j, 0)),
                  pl.BlockSpec((g, c, w3), lambda i, j: (i, nc - 1 - j, 0)),
                  pl.BlockSpec((c, w), lambda i, j: (nc - 1 - j, 0)),
                  pl.BlockSpec((c, w), lambda i, j: (nc - 1 - j, 0)),
                  const3, const3,
                  pl.BlockSpec(dec.shape, lambda i, j: (0, 0, 0, 0)),
                  st],
        out_specs=[pl.BlockSpec((g, c, w), lambda i, j: (i, j, 0)),
                   pl.BlockSpec((g, c, w), lambda i, j: (i, nc - 1 - j, 0)),
                   st],
        out_shape=[jax.ShapeDtypeStruct((b, l, w), F32)] * 2 + [jax.ShapeDtypeStruct(s0.shape, F32)],
        scratch_shapes=[pltpu.VMEM((2, g, n_pairs, PAIR, PAIR), F32)],
        compiler_params=_params("arbitrary", "arbitrary"),
        name="ret_scan",
    )(qkv, cos, sin, qkv, cos, sin, inner_f, inner_b, dec, s0)


def _ret_tables(n_heads):
    c = RET_CHUNK
    lg = np.log(1.0 - 2.0 ** (-5.0 - np.arange(n_heads, dtype=np.float64)))
    idx = np.arange(c, dtype=np.float64)
    diff = idx[:, None] - idx[None, :]

    def inner(log_gamma, reverse):
        dd = -diff if reverse else diff
        m = np.where(dd >= 0, np.exp(log_gamma[:, None, None] * np.maximum(dd, 0.0)), 0.0)
        return m.reshape(n_heads // 2, 2, c, c).transpose(0, 2, 1, 3).reshape(n_heads // 2, c, 2 * c)

    def per_lane(tab):
        return np.repeat(tab, HEAD_DIM, axis=1)

    dec = []
    for reverse in (False, True):
        g = lg[::-1] if reverse else lg
        pos = (c - 1.0 - idx) if reverse else idx
        qd = per_lane(np.exp(g[None, :] * (pos[:, None] + 1.0)))
        kd = per_lane(np.exp(g[None, :] * (c - 1.0 - pos[:, None])))
        cd = per_lane(np.broadcast_to(np.exp(g * c)[None, :], (c, n_heads)))
        dec.append(np.stack([qd, kd, cd]))
    return (jnp.asarray(inner(lg, False), F32), jnp.asarray(inner(lg[::-1], True), F32),
            jnp.asarray(np.stack(dec), F32))


def _rope_tables(l, n_heads, rotate):
    if not rotate:
        return jnp.ones((l, n_heads * HEAD_DIM), F32), jnp.zeros((l, n_heads * HEAD_DIM), F32)
    rows = l // GRID_W
    row = jnp.repeat(jnp.arange(rows, dtype=F32), GRID_W)
    col = jnp.tile(jnp.arange(GRID_W, dtype=F32), rows)
    nf = HEAD_DIM // 4
    freqs = ROPE_BASE ** (-jnp.arange(nf, dtype=F32) / nf)
    ang = jnp.concatenate([row[:, None] * freqs, col[:, None] * freqs], axis=-1)
    cos = jnp.tile(jnp.cos(ang), (1, 2 * n_heads))
    sin = jnp.tile(jnp.concatenate([-jnp.sin(ang), jnp.sin(ang)], axis=-1), (1, n_heads))
    return cos, sin


def _group_norm(y, ones, w, b, eps):
    inv_n = 1.0 / HEAD_DIM
    mu = _dot(y.astype(BF16), ones) * inv_n
    d = y - mu
    var = _dot((d * d).astype(BF16), ones) * inv_n
    return d * lax.rsqrt(var + eps) * w + b


def _out_ffn_kernel(x_ref, yf_ref, yb_ref, bonus_ref, gate_ref, of_ref, ob_ref, pg_ref,
                    lnw_ref, lnb_ref, gnw_ref, gnb_ref, ones_ref, wo_ref,
                    g1_ref, sh2_ref, sc2_ref, g2_ref, n2_ref, nf_ref, wfi_ref, wfo_ref,
                    o_ref, *, width, hidden, th, final_norm):
    ones = ones_ref[...]
    w = width
    o_rwkv = (_group_norm(yf_ref[0] + yb_ref[0], ones, lnw_ref[...], lnb_ref[...], RWKV_GN_EPS)
              + bonus_ref[0]) * gate_ref[0]
    pg = pg_ref[0]
    o_ret = (_group_norm(of_ref[0], ones, gnw_ref[...], gnb_ref[...], RET_GN_EPS) * _silu(pg[:, :w])
             + _group_norm(ob_ref[0], ones, gnw_ref[...], gnb_ref[...], RET_GN_EPS) * _silu(pg[:, w:]))
    attn = _dot(o_rwkv.astype(BF16), wo_ref[:w]) + _dot(o_ret.astype(BF16), wo_ref[w:])
    x1 = x_ref[0] + g1_ref[0] * attn
    h = _modnorm(x1, n2_ref[...], sh2_ref[0], sc2_ref[0]).astype(BF16)
    acc = None
    for c0 in range(0, hidden, th):
        gate = _dot(h, wfi_ref[:, c0:c0 + th])
        up = _dot(h, wfi_ref[:, hidden + c0:hidden + c0 + th])
        part = _dot((_silu(gate) * up).astype(BF16), wfo_ref[c0:c0 + th])
        acc = part if acc is None else acc + part
    x2 = x1 + g2_ref[0] * acc
    if final_norm:
        ms = jnp.mean(x2 * x2, axis=-1, keepdims=True)
        x2 = (x2 * lax.rsqrt(ms + NORM_EPS)) * nf_ref[...]
    o_ref[0] = x2


def _out_ffn(x, yf, yb, bonus, gate, of, ob, pg, consts, mods, final_norm, tm):
    b, l, d = x.shape
    w = yf.shape[-1]
    hidden = consts["w_ffn_out"].shape[0]
    th = hidden // 2 if (hidden // 2) % 128 == 0 else hidden
    bm = mods[0].shape[0]
    mod_map = (lambda i, j: (i, 0, 0)) if bm > 1 else (lambda i, j: (0, 0, 0))
    tok = lambda width: pl.BlockSpec((1, tm, width), lambda i, j: (i, j, 0))
    const = lambda arr: pl.BlockSpec(arr.shape, lambda i, j: (0,) * arr.ndim, pipeline_mode=pl.Buffered(1))
    names = ("ln_w", "ln_b", "gn_w", "gn_b", "ones", "w_out")
    tail = ("norm2", "norm_f", "w_ffn_in", "w_ffn_out")
    in_specs = ([tok(d)] + [tok(w)] * 6 + [tok(2 * w)] + [const(consts[n]) for n in names]
                + [pl.BlockSpec((1, 1, d), mod_map)] * 4 + [const(consts[n]) for n in tail])
    args = ([x, yf, yb, bonus, gate, of, ob, pg] + [consts[n] for n in names] + list(mods)
            + [consts[n] for n in tail])
    kern = functools.partial(_out_ffn_kernel, width=w, hidden=hidden, th=th, final_norm=final_norm)
    return pl.pallas_call(
        kern,
        grid=(b, l // tm),
        in_specs=in_specs,
        out_specs=tok(d),
        out_shape=jax.ShapeDtypeStruct((b, l, d), F32),
        compiler_params=_params("arbitrary", "arbitrary"),
        name="out_ffn",
    )(*args)


def _pad_rows(a, rows, offset=0):
    out = jnp.zeros((rows,) + a.shape[1:], a.dtype)
    return out.at[offset:offset + a.shape[0]].set(a)


def _layer_consts(i, w_in, w_vres_down, mu_rwkv, mu_vres, w0, w_up, a0, a_up, g_up, k_k, k_a, r_k, v0, v_up,
                  ln_x_w, ln_x_b, ret_gn_w, ret_gn_b, w_out, w_ffn_in, w_ffn_out, norm2, norm_f):
    width = w0.shape[-1]
    n_decay, n_iclr, n_gate = w_up.shape[2], a_up.shape[2], g_up.shape[1]
    n_vres = v_up.shape[1]
    n_rwkv = 3 * width + n_decay + n_iclr + n_gate
    assert n_decay + n_iclr == 128 and n_gate + n_vres <= 256
    rwkv_cols = 3 * width + 128 + 256
    d = w_in.shape[1]
    has_vres = i > 0
    wi = w_in[i]
    pad = jnp.zeros((d, rwkv_cols - n_rwkv - n_vres), F32)
    vres_cols = w_vres_down[i - 1] if has_vres else jnp.zeros((d, n_vres), F32)
    w_proj = jnp.concatenate([wi[:, :n_rwkv], vres_cols, pad, wi[:, n_rwkv:]], axis=1).astype(BF16)
    mu_v = mu_vres[i - 1] if has_vres else jnp.zeros((n_vres,), F32)
    mu = jnp.concatenate([mu_rwkv[i], mu_v, jnp.zeros((rwkv_cols - n_rwkv - n_vres,), F32)])[None]
    cat2 = lambda a: jnp.concatenate([a[0], a[1]], axis=-1)
    head = jnp.arange(width) // HEAD_DIM
    ones = (head[:, None] == head[None, :]).astype(BF16)
    row = lambda a: a.reshape(1, -1)
    return dict(
        w_proj=w_proj, widths=(rwkv_cols, 3 * width, 2 * width), n_rwkv=n_rwkv, n_vres=n_vres, mu=mu,
        w0=row(cat2(w0[i])), w_up=_pad_rows(cat2(w_up[i]), 128), a0=row(cat2(a0[i])),
        a_up=_pad_rows(cat2(a_up[i]), 128, n_decay), g_up=_pad_rows(g_up[i], 256),
        v_up=_pad_rows(v_up[i - 1] if has_vres else jnp.zeros((n_vres, width), F32), 256, n_gate),
        v0=row(v0[i - 1] if has_vres else jnp.zeros((width,), F32)),
        k_k=row(k_k[i]), k_a=row(k_a[i]), r_k=row(r_k[i]), ones=ones,
        ln_w=row(ln_x_w[i]), ln_b=row(ln_x_b[i]), gn_w=row(ret_gn_w[i]), gn_b=row(ret_gn_b[i]),
        w_out=w_out[i].astype(BF16), w_ffn_in=w_ffn_in[i].astype(BF16), w_ffn_out=w_ffn_out[i].astype(BF16),
        norm2=row(norm2[i]), norm_f=row(norm_f),
    )


def _mixers(x, norm1, shift, scale, consts, vfirst, s_rwkv, s_ret, rope, tables, quad, tm):
    p_rwkv, p_qkv, p_g = _in_proj(x, norm1, shift, scale, consts["w_proj"], consts["widths"], tm)
    r, v, kk, bonus, gate, lwf, lwb, kf, kb, bf, bb = _rwkv_prep(
        p_rwkv, vfirst, consts, quad, min(TM_PREP, x.shape[1]) if quad else x.shape[1])
    yf, yb, s_rwkv = _rwkv_scan(r, v, kk, lwf, lwb, kf, kb, bf, bb, s_rwkv)
    of, ob, s_ret = _ret_scan(p_qkv, rope[0], rope[1], tables, s_ret)
    return (yf, yb, bonus, gate, of, ob, p_g), v, s_rwkv, s_ret


def kernel(x, c, ctx, c_ctx, w_ada, b_ada, norm1, norm2, norm_f, w_in, w_vres_down, mu_rwkv, mu_vres, w0, w_up, a0, a_up, g_up, k_k, k_a, r_k, v0, v_up, ln_x_w, ln_x_b, ret_gn_w, ret_gn_b, w_out, w_ffn_in, w_ffn_out):
    b, l, d = x.shape
    l_ctx = ctx.shape[1]
    depth = w_in.shape[0]
    width = w0.shape[-1]
    n_heads = width // HEAD_DIM
    n_pairs = n_heads // 2
    tm = min(TM_PROJ, l)
    tm_ctx = min(TM_PROJ, l_ctx)
    tm_ffn = min(TM_FFN, l)
    tm_ffn_ctx = min(TM_FFN, l_ctx)

    m_pad = -(-(b + 1) // 8) * 8
    cond = jnp.zeros((m_pad, d), F32).at[:b].set(c).at[b].set(c_ctx)
    mod = _ada(cond, w_ada, b_ada)

    tables = _ret_tables(n_heads)
    rope_l = _rope_tables(l, n_heads, True)
    rope_c = _rope_tables(l_ctx, n_heads, False)
    zero_state = jnp.zeros((2, b, n_pairs, PAIR, PAIR), F32)

    vfirst_c = vfirst_l = None
    for i in range(depth):
        last = i == depth - 1
        consts = _layer_consts(i, w_in, w_vres_down, mu_rwkv, mu_vres, w0, w_up, a0, a_up, g_up, k_k, k_a, r_k,
                               v0, v_up, ln_x_w, ln_x_b, ret_gn_w, ret_gn_b, w_out, w_ffn_in, w_ffn_out,
                               norm2, norm_f)
        mods_l = [mod[i, :b, None, j * d:(j + 1) * d] for j in range(6)]
        mods_c = [mod[i, b:b + 1, None, j * d:(j + 1) * d] for j in range(6)]
        n1 = norm1[i][None]

        mix_c, v_c, s_rwkv, s_ret = _mixers(ctx, n1, mods_c[0], mods_c[1], consts, vfirst_c,
                                            zero_state, zero_state, rope_c, tables, False, tm_ctx)
        mix_l, v_l, _, _ = _mixers(x, n1, mods_l[0], mods_l[1], consts, vfirst_l,
                                   s_rwkv, s_ret, rope_l, tables, True, tm)
        if i == 0:
            vfirst_c, vfirst_l = v_c, v_l
        x = _out_ffn(x, *mix_l, consts, (mods_l[2], mods_l[3], mods_l[4], mods_l[5]), last, tm_ffn)
        if not last:
            ctx = _out_ffn(ctx, *mix_c, consts, (mods_c[2], mods_c[3], mods_c[4], mods_c[5]), False, tm_ffn_ctx)
    return x
```

```python
import functools

import numpy as np
import jax
import jax.numpy as jnp
from jax import lax
from jax.experimental import pallas as pl
from jax.experimental.pallas import tpu as pltpu

F32 = jnp.float32
BF16 = jnp.bfloat16

GRID_W = 64
HEAD_DIM = 64
PAIR = 2 * HEAD_DIM
RWKV_GROUP = 2 * HEAD_DIM
RWKV_CHUNK = 64
RET_CHUNK = 128
ROPE_BASE = 10000.0
NORM_EPS = 1e-6
RWKV_GN_EPS = 64e-5
RET_GN_EPS = 1e-5
VMEM_LIMIT = 56 * 1024 * 1024
TM_PROJ = 512
TM_PREP = 256
TM_FFN = 512
SCAN_BATCH = 4
LORA_PASSES = 1
HEAD_SUM_PIECES = 2
RWKV_PASSES = dict(state=1, gram=1, inverse=1, kv=1, solve=1, out=1, update=1)

NN = ((1,), (0,))
NT = ((1,), (1,))


def _dot(a, b, dims=NN):
    return lax.dot_general(a, b, (dims, ((), ())), preferred_element_type=F32)


def _split(a):
    hi = a.astype(BF16)
    lo = (a - hi.astype(F32)).astype(BF16)
    return hi, lo


def _mm(a, b, dims=NN, passes=1):
    if passes == 1:
        return _dot(a.astype(BF16), b.astype(BF16), dims)
    ah, al = _split(a)
    bh, bl = _split(b)
    lhs = jnp.concatenate([ah, al, ah], axis=1)
    rhs = jnp.concatenate([bh, bh, bl], axis=0 if dims == NN else 1)
    return _dot(lhs, rhs, dims)


def _mm_exact_lhs(a_bf16, b, pieces=3):
    acc = None
    rem = b
    for _ in range(pieces):
        p = rem.astype(BF16)
        t = _dot(a_bf16, p)
        acc = t if acc is None else acc + t
        rem = rem - p.astype(F32)
    return acc


def _mm_exact_rhs(a, b_bf16, pieces=3):
    acc = None
    rem = a
    for _ in range(pieces):
        p = rem.astype(BF16)
        t = _dot(p, b_bf16)
        acc = t if acc is None else acc + t
        rem = rem - p.astype(F32)
    return acc


def _sigmoid(x):
    return 1.0 / (1.0 + jnp.exp(-x))


def _silu(x):
    return x * _sigmoid(x)


def _softplus(z):
    return jnp.maximum(z, 0.0) + jnp.log(1.0 + jnp.exp(-jnp.abs(z)))


def _params(*sem):
    return pltpu.CompilerParams(dimension_semantics=sem, vmem_limit_bytes=VMEM_LIMIT)


def _ada_kernel(c_ref, w_ref, b_ref, o_ref):
    o_ref[0] = _mm(_silu(c_ref[...]), w_ref[0], passes=3) + b_ref[0]


def _ada(cond, w_ada, b_ada):
    depth, d, n = w_ada.shape
    m = cond.shape[0]
    tn = n // 4
    return pl.pallas_call(
        _ada_kernel,
        grid=(depth, n // tn),
        in_specs=[pl.BlockSpec((m, d), lambda i, j: (0, 0)),
                  pl.BlockSpec((1, d, tn), lambda i, j: (i, 0, j)),
                  pl.BlockSpec((1, 1, tn), lambda i, j: (i, 0, j))],
        out_specs=pl.BlockSpec((1, m, tn), lambda i, j: (i, 0, j)),
        out_shape=jax.ShapeDtypeStruct((depth, m, n), F32),
        compiler_params=_params("arbitrary", "arbitrary"),
        name="ada",
    )(cond, w_ada, b_ada.reshape(depth, 1, n))


def _modnorm(x, g, shift, scale):
    ms = jnp.mean(x * x, axis=-1, keepdims=True)
    y = (x * lax.rsqrt(ms + NORM_EPS)) * g
    return y * (1.0 + scale) + shift


def _in_proj_kernel(*refs, widths, tn, rope):
    x_ref, g_ref, sh_ref, sc_ref, w_ref = refs[:5]
    cos_ref, sin_ref = refs[5:7] if rope else (None, None)
    out_refs = refs[7:] if rope else refs[5:]
    h = _modnorm(x_ref[0], g_ref[...], sh_ref[0], sc_ref[0]).astype(BF16)
    col = 0
    for idx, (o_ref, width) in enumerate(zip(out_refs, widths)):
        for c0 in range(0, width, tn):
            c1 = min(c0 + tn, width)
            y = _dot(h, w_ref[:, col + c0:col + c1])
            if idx == 1 and c1 <= 2 * (width // 3):
                if rope:
                    y = _rope(y, cos_ref[...], sin_ref[...])
                if c0 >= width // 3:
                    y = y * HEAD_DIM ** -0.5
            o_ref[0, :, c0:c1] = y
        col += width


def _in_proj(x, g, shift, scale, w, widths, rope, tm):
    b, l, d = x.shape
    bm = shift.shape[0]
    tn = widths[1] // 3
    mod_map = (lambda i, j: (i, 0, 0)) if bm > 1 else (lambda i, j: (0, 0, 0))
    kern = functools.partial(_in_proj_kernel, widths=widths, tn=tn, rope=rope is not None)
    in_specs = [pl.BlockSpec((1, tm, d), lambda i, j: (i, j, 0)),
                pl.BlockSpec((1, d), lambda i, j: (0, 0)),
                pl.BlockSpec((1, 1, d), mod_map),
                pl.BlockSpec((1, 1, d), mod_map),
                pl.BlockSpec(w.shape, lambda i, j: (0, 0))]
    args = [x, g, shift, scale, w]
    if rope is not None:
        in_specs += [pl.BlockSpec((tm, tn), lambda i, j: (j, 0))] * 2
        args += list(rope)
    return pl.pallas_call(
        kern,
        grid=(b, l // tm),
        in_specs=in_specs,
        out_specs=[pl.BlockSpec((1, tm, wd), lambda i, j: (i, j, 0)) for wd in widths],
        out_shape=[jax.ShapeDtypeStruct((b, l, wd), F32) for wd in widths],
        compiler_params=_params("arbitrary", "arbitrary"),
        name="in_proj",
    )(*args)


def _direction_segments(n_cols, parts, offset, width):
    q = n_cols // parts
    return [(offset + i * q, offset + (i + 1) * q, i) for i in range(parts)]


def _shift_lerp(x, mu_ref, shifted_fn, segments, n_cols):
    outs = []
    for g0 in range(0, n_cols, 128):
        xg = x[:, g0:g0 + 128]
        lane = lax.broadcasted_iota(jnp.int32, xg.shape, 1) + g0
        s = None
        for (c0, c1, d) in segments:
            lo, hi = max(c0, g0), min(c1, g0 + 128)
            if lo >= hi:
                continue
            sd = shifted_fn(xg, d, g0)
            if lo == g0 and hi == g0 + 128:
                s = sd
            else:
                part = jnp.where((lane >= lo) & (lane < hi), sd, 0.0)
                s = part if s is None else s + part
        if s is None:
            s = jnp.zeros_like(xg)
        outs.append(xg + (s - xg) * mu_ref[:, g0:g0 + 128])
    return jnp.concatenate(outs, axis=1)


def _rwkv_prep_kernel(*refs, quad, has_vres, segments, width, n_cols):
    it = iter(refs)
    cur_ref = next(it)
    prev_ref = next(it) if quad else None
    next_ref = next(it) if quad else None
    vfirst_ref = next(it) if has_vres else None
    (mu_ref, w0_ref, wup_ref, a0_ref, aup_ref, gup_ref, vup_ref, v0_ref, kk_ref, ka_ref, rk_ref,
     ones_ref) = (next(it) for _ in range(12))
    (r_o, v_o, kk_o, bonus_o, gate_o, lwf_o, lwb_o, kf_o, kb_o, bf_o, bb_o) = (next(it) for _ in range(11))

    x = cur_ref[0]
    t = x.shape[0]
    row = lax.broadcasted_iota(jnp.int32, (t, 128), 0)
    if quad:
        gcol = row % GRID_W
        first = pl.program_id(1) == 0
        last = pl.program_id(1) == pl.num_programs(1) - 1

        def shifted(xg, d, g0):
            if d == 0:
                return jnp.where(gcol == 0, 0.0, pltpu.roll(xg, 1, 0))
            if d == 1:
                return jnp.where(gcol == GRID_W - 1, 0.0, pltpu.roll(xg, t - 1, 0))
            if d == 2:
                halo = jnp.where(first, 0.0, prev_ref[0, :, g0:g0 + 128])
                return jnp.concatenate([halo, xg[:t - GRID_W]], axis=0)
            halo = jnp.where(last, 0.0, next_ref[0, :, g0:g0 + 128])
            return jnp.concatenate([xg[GRID_W:], halo], axis=0)
    else:
        def shifted(xg, d, g0):
            if d == 0:
                return jnp.where(row == 0, 0.0, pltpu.roll(xg, 1, 0))
            return jnp.where(row == t - 1, 0.0, pltpu.roll(xg, t - 1, 0))

    u = _shift_lerp(x, mu_ref, shifted, segments, n_cols)
    w = width
    r, k, v = u[:, :w], u[:, w:2 * w], u[:, 2 * w:3 * w]
    lo_wa = u[:, 3 * w:3 * w + 128]
    lo_gv = u[:, 3 * w + 128:3 * w + 384]

    lp = LORA_PASSES
    logw = -jnp.exp(-_softplus(-(w0_ref[...] + _mm(jnp.tanh(lo_wa), wup_ref[...], passes=lp))) - 0.5)
    iclr = _sigmoid(a0_ref[...] + _mm(lo_wa, aup_ref[...], passes=lp))
    gate = _mm(_sigmoid(lo_gv), gup_ref[...], passes=lp)
    if has_vres:
        mix = _sigmoid(v0_ref[...] + _mm(lo_gv, vup_ref[...], passes=lp))
        v = v + (vfirst_ref[0] - v) * mix

    ones = ones_ref[...]
    kk = k * kk_ref[...]
    kk = kk * lax.rsqrt(jnp.maximum(_mm_exact_rhs(kk * kk, ones, HEAD_SUM_PIECES), 1e-24))
    bonus = _mm_exact_rhs(r * k * rk_ref[...], ones, HEAD_SUM_PIECES) * v

    r_o[0], v_o[0], kk_o[0], bonus_o[0], gate_o[0] = r, v, kk, bonus, gate
    ka = ka_ref[...]
    for d, (lw_o, k_o, b_o) in enumerate(((lwf_o, kf_o, bf_o), (lwb_o, kb_o, bb_o))):
        a = iclr[:, d * w:(d + 1) * w]
        lw_o[0] = logw[:, d * w:(d + 1) * w]
        k_o[0] = k * (1.0 + (a - 1.0) * ka)
        b_o[0] = kk * a


def _rwkv_prep(p, vfirst, consts, quad, tm):
    b, l, n_cols = p.shape
    w = consts["k_k"].shape[-1]
    has_vres = vfirst is not None
    n_rwkv = consts["n_rwkv"]
    n_vres = consts["n_vres"]
    parts = 4 if quad else 2
    segments = _direction_segments(n_rwkv, parts, 0, w) + _direction_segments(n_vres, parts, n_rwkv, w)
    kern = functools.partial(_rwkv_prep_kernel, quad=quad, has_vres=has_vres, segments=segments,
                             width=w, n_cols=n_cols)
    rows = tm // GRID_W
    n_rows = l // GRID_W
    in_specs = [pl.BlockSpec((1, tm, n_cols), lambda i, j: (i, j, 0))]
    args = [p]
    if quad:
        in_specs += [
            pl.BlockSpec((1, GRID_W, n_cols), lambda i, j: (i, jnp.maximum(j * rows - 1, 0), 0)),
            pl.BlockSpec((1, GRID_W, n_cols), lambda i, j: (i, jnp.minimum((j + 1) * rows, n_rows - 1), 0)),
        ]
        args += [p, p]
    if has_vres:
        in_specs.append(pl.BlockSpec((1, tm, w), lambda i, j: (i, j, 0)))
        args.append(vfirst)
    for name in ("mu", "w0", "w_up", "a0", "a_up", "g_up", "v_up", "v0", "k_k", "k_a", "r_k", "ones"):
        arr = consts[name]
        in_specs.append(pl.BlockSpec(arr.shape, lambda i, j: (0, 0)))
        args.append(arr)
    return pl.pallas_call(
        kern,
        grid=(b, l // tm),
        in_specs=in_specs,
        out_specs=[pl.BlockSpec((1, tm, w), lambda i, j: (i, j, 0))] * 11,
        out_shape=[jax.ShapeDtypeStruct((b, l, w), F32)] * 11,
        compiler_params=_params("arbitrary", "arbitrary"),
        name="rwkv_prep",
    )(*args)


def _block_diag(y, block=HEAD_DIM):
    blk = lax.broadcasted_iota(jnp.int32, y.shape, 1) // block
    return jnp.concatenate([jnp.where(blk == h, y, 0.0) for h in range(y.shape[1] // block)], axis=0)


def _head_diag_mask(width):
    r = lax.broadcasted_iota(jnp.int32, (width, width), 0) // HEAD_DIM
    c = lax.broadcasted_iota(jnp.int32, (width, width), 1) // HEAD_DIM
    return r == c


def _rwkv_stream(r, v, kk, lw, kd, bd, reverse, state, out):
    c = r.shape[0]
    t_idx = lax.broadcasted_iota(jnp.int32, (c, c), 0)
    s_idx = lax.broadcasted_iota(jnp.int32, (c, c), 1)
    tri = jnp.where((s_idx >= t_idx) if reverse else (s_idx <= t_idx), 1.0, 0.0).astype(BF16)
    lc = _mm_exact_lhs(tri, lw)
    l_end = lc[0:1] if reverse else lc[c - 1:c]
    e_inv = jnp.exp(-lc)
    to_end = jnp.exp(l_end - lc)
    heads = RWKV_GROUP // HEAD_DIM
    row = lax.broadcasted_iota(jnp.int32, (c, heads * c), 0)
    col = lax.broadcasted_iota(jnp.int32, (c, heads * c), 1) % c
    return dict(
        state=state, out=out,
        v=v, r_t=r * jnp.exp(lc), a_t=-kk * jnp.exp(lc - lw), k_t=kd * e_inv, b_t=bd * e_inv,
        k_hat=kd * to_end, b_hat=bd * to_end, p_end=jnp.exp(l_end),
        before=(col > row) if reverse else (col < row),
        upto=(col >= row) if reverse else (col <= row),
        eye=jnp.where(col == row, 1.0, 0.0))


def _rwkv_step(streams, s_scr, passes):
    c = streams[0]["v"].shape[0]
    gw = RWKV_GROUP
    chains = [(k, p) for k in range(len(streams)) for p in range(streams[0]["v"].shape[1] // gw)]
    sl = lambda p: slice(p * gw, (p + 1) * gw)
    diag = _head_diag_mask(gw)

    z, zs, s0 = {}, {}, {}
    for ch in chains:
        st, p = streams[ch[0]], ch[1]
        z[ch] = jnp.concatenate([st["a_t"][:, sl(p)], st["r_t"][:, sl(p)]], axis=0)
        s0[ch] = s_scr[st["state"] + (p,)]
        zs[ch] = _mm(z[ch], s0[ch], NT, passes["state"])
    l_ab, m_rb, l_ak, m_rk = {}, {}, {}, {}
    for ch in chains:
        st, p = streams[ch[0]], ch[1]
        w_bk = jnp.concatenate([_block_diag(st["b_t"][:, sl(p)]), _block_diag(st["k_t"][:, sl(p)])], axis=0)
        gram = _mm(z[ch], w_bk, NT, passes["gram"])
        nb = gram.shape[1] // 2
        l_ab[ch] = jnp.where(st["before"], gram[:c, :nb], 0.0)
        m_rb[ch] = jnp.where(st["upto"], gram[c:, :nb], 0.0)
        l_ak[ch] = jnp.where(st["before"], gram[:c, nb:], 0.0)
        m_rk[ch] = jnp.where(st["upto"], gram[c:, nb:], 0.0)

    inv = {ch: streams[ch[0]]["eye"] + l_ab[ch] for ch in chains}
    power = {ch: _mm(l_ab[ch], _block_diag(l_ab[ch], c), NN, passes["inverse"]) for ch in chains}
    kv = {ch: _mm(jnp.concatenate([l_ak[ch], m_rk[ch]], axis=0),
                  _block_diag(streams[ch[0]]["v"][:, sl(ch[1])]), NN, passes["kv"]) for ch in chains}
    for _ in range(int(np.log2(c)) - 2):
        for ch in chains:
            w_ip = jnp.concatenate([_block_diag(inv[ch], c), _block_diag(power[ch], c)], axis=1)
            prod = _mm(power[ch], w_ip, NN, passes["inverse"])
            nb = prod.shape[1] // 2
            inv[ch] = inv[ch] + prod[:, :nb]
            power[ch] = prod[:, nb:]
    for ch in chains:
        inv[ch] = inv[ch] + _mm(power[ch], _block_diag(inv[ch], c), NN, passes["inverse"])

    u = {ch: _mm(inv[ch], _block_diag(zs[ch][:c] + kv[ch][:c]), NN, passes["solve"]) for ch in chains}
    for ch in chains:
        st, p = streams[ch[0]], ch[1]
        y_ref, g = st["out"]
        y_ref[g, :, sl(p)] = zs[ch][c:] + kv[ch][c:] + _mm(m_rb[ch], _block_diag(u[ch]), NN, passes["out"])
    for ch in chains:
        st, p = streams[ch[0]], ch[1]
        uv_t =jnp.concatenate([u[ch], st["v"][:, sl(p)]], axis=0).T
        upd = _mm(uv_t, jnp.concatenate([st["b_hat"][:, sl(p)], st["k_hat"][:, sl(p)]], axis=0), NN,
                  passes["update"])
        s_scr[st["state"] + (p,)] = s0[ch] * st["p_end"][:, sl(p)] + jnp.where(diag, upd, 0.0)


def _rwkv_scan_kernel(rf, vf, kkf, lwf, kf, bf, rb, vb, kkb, lwb, kb, bb, s0_ref,
                      yf_ref, yb_ref, sout_ref, s_scr, *, passes):
    i = pl.program_id(1)

    @pl.when(i == 0)
    def _():
        s_scr[...] = s0_ref[...]

    streams = []
    for g in range(rf.shape[0]):
        streams.append(_rwkv_stream(rf[g], vf[g], kkf[g], lwf[g], kf[g], bf[g], False, (0, g), (yf_ref, g)))
        streams.append(_rwkv_stream(rb[g], vb[g], kkb[g], lwb[g], kb[g], bb[g], True, (1, g), (yb_ref, g)))
    _rwkv_step(streams, s_scr, passes)

    @pl.when(i == pl.num_programs(1) - 1)
    def _():
        sout_ref[...] = s_scr[...]


def _rwkv_scan(r, v, kk, lwf, lwb, kf, kb, bf, bb, s0):
    b, l, w = r.shape
    c = RWKV_CHUNK
    nc = l // c
    g = SCAN_BATCH if b % SCAN_BATCH == 0 else 1
    fwd = pl.BlockSpec((g, c, w), lambda i, j: (i, j, 0))
    bwd = pl.BlockSpec((g, c, w), lambda i, j: (i, nc - 1 - j, 0))
    st = pl.BlockSpec((2, g) + s0.shape[2:], lambda i, j: (0, i, 0, 0, 0))
    return pl.pallas_call(
        functools.partial(_rwkv_scan_kernel, passes=dict(RWKV_PASSES)),
        grid=(b // g, nc),
        in_specs=[fwd] * 6 + [bwd] * 6 + [st],
        out_specs=[fwd, bwd, st],
        out_shape=[jax.ShapeDtypeStruct((b, l, w), F32)] * 2 + [jax.ShapeDtypeStruct(s0.shape, F32)],
        scratch_shapes=[pltpu.VMEM((2, g) + s0.shape[2:], F32)],
        compiler_params=_params("arbitrary", "arbitrary"),
        name="rwkv_scan",
    )(r, v, kk, lwf, kf, bf, r, v, kk, lwb, kb, bb, s0)


def _rope(x, cos, sin_signed):
    w = x.shape[1]
    half = HEAD_DIM // 2
    lane = lax.broadcasted_iota(jnp.int32, x.shape, 1) % HEAD_DIM
    rot = jnp.where(lane < half, pltpu.roll(x, w - half, 1), pltpu.roll(x, half, 1))
    return x * cos + rot * sin_signed


def _ret_step(streams, s_scr):
    n_pairs = streams[0]["v"].shape[1] // PAIR
    chains = [(k, p) for k in range(len(streams)) for p in range(n_pairs)]
    sl = lambda p: slice(p * PAIR, (p + 1) * PAIR)
    diag = _head_diag_mask(PAIR)
    scores, cross, s0 = {}, {}, {}
    for ch in chains:
        st, p = streams[ch[0]], ch[1]
        s0[ch] = s_scr[st["state"] + (p,)]
        scores[ch] = _mm(st["q"][:, sl(p)], _block_diag(st["k"][:, sl(p)]), NT) * st["inner"][p]
        cross[ch] = _mm(st["q"][:, sl(p)], s0[ch]) * st["qdec"][:, sl(p)]
    for ch in chains:
        st, p = streams[ch[0]], ch[1]
        o_ref, g = st["out"]
        o_ref[g, :, sl(p)] = _mm(scores[ch], _block_diag(st["v"][:, sl(p)])) + cross[ch]
    for ch in chains:
        st, p = streams[ch[0]], ch[1]
        upd = _mm((st["k"][:, sl(p)] * st["kdec"][:, sl(p)]).T, st["v"][:, sl(p)])
        s_scr[st["state"] + (p,)] = s0[ch] * st["cdec"][:, sl(p)] + jnp.where(diag, upd, 0.0)


def _ret_scan_kernel(qkvf, qkvb, inf_ref, inb_ref, dec_ref, s0_ref,
                     of_ref, ob_ref, sout_ref, s_scr, *, width):
    i = pl.program_id(1)

    @pl.when(i == 0)
    def _():
        s_scr[...] = s0_ref[...]

    w = width
    streams = []
    for g in range(qkvf.shape[0]):
        for d, (qkv, inner_ref, o_ref) in enumerate(((qkvf, inf_ref, of_ref), (qkvb, inb_ref, ob_ref))):
            x = qkv[g]
            streams.append(dict(
                q=x[:, :w], k=x[:, w:2 * w], v=x[:, 2 * w:], inner=inner_ref, qdec=dec_ref[d, 0],
                kdec=dec_ref[d, 1], cdec=dec_ref[d, 2, 0:1], state=(d, g), out=(o_ref, g)))
    _ret_step(streams, s_scr)

    @pl.when(i == pl.num_programs(1) - 1)
    def _():
        sout_ref[...] = s_scr[...]


def _ret_scan(qkv, tables, s0):
    b, l, w3 = qkv.shape
    w = w3 // 3
    c = RET_CHUNK
    nc = l // c
    n_pairs = w // PAIR
    g = SCAN_BATCH if b % SCAN_BATCH == 0 else 1
    inner_f, inner_b, dec = tables
    st = pl.BlockSpec((2, g, n_pairs, PAIR, PAIR), lambda i, j: (0, i, 0, 0, 0))
    const3 = pl.BlockSpec(inner_f.shape, lambda i, j: (0, 0, 0))
    return pl.pallas_call(
        functools.partial(_ret_scan_kernel, width=w),
        grid=(b // g, nc),
        in_specs=[pl.BlockSpec((g, c, w3), lambda i, j: (i, j, 0)),
                  pl.BlockSpec((g, c, w3), lambda i, j: (i, nc - 1 - j, 0)),
                  const3, const3,
                  pl.BlockSpec(dec.shape, lambda i, j: (0, 0, 0, 0)),
                  st],
        out_specs=[pl.BlockSpec((g, c, w), lambda i, j: (i, j, 0)),
                   pl.BlockSpec((g, c, w), lambda i, j: (i, nc - 1 - j, 0)),
                   st],
        out_shape=[jax.ShapeDtypeStruct((b, l, w), F32)] * 2 + [jax.ShapeDtypeStruct(s0.shape, F32)],
        scratch_shapes=[pltpu.VMEM((2, g, n_pairs, PAIR, PAIR), F32)],
        compiler_params=_params("arbitrary", "arbitrary"),
        name="ret_scan",
    )(qkv, qkv, inner_f, inner_b, dec, s0)


def _ret_tables(n_heads):
    c = RET_CHUNK
    lg = np.log(1.0 - 2.0 ** (-5.0 - np.arange(n_heads, dtype=np.float64)))
    idx = np.arange(c, dtype=np.float64)
    diff = idx[:, None] - idx[None, :]

    def inner(log_gamma, reverse):
        dd = -diff if reverse else diff
        m = np.where(dd >= 0, np.exp(log_gamma[:, None, None] * np.maximum(dd, 0.0)), 0.0)
        return m.reshape(n_heads // 2, 2, c, c).transpose(0, 2, 1, 3).reshape(n_heads // 2, c, 2 * c)

    def per_lane(tab):
        return np.repeat(tab, HEAD_DIM, axis=1)

    dec = []
    for reverse in (False, True):
        g = lg[::-1] if reverse else lg
        pos = (c - 1.0 - idx) if reverse else idx
        qd = per_lane(np.exp(g[None, :] * (pos[:, None] + 1.0)))
        kd = per_lane(np.exp(g[None, :] * (c - 1.0 - pos[:, None])))
        cd = per_lane(np.broadcast_to(np.exp(g * c)[None, :], (c, n_heads)))
        dec.append(np.stack([qd, kd, cd]))
    return (jnp.asarray(inner(lg, False), F32), jnp.asarray(inner(lg[::-1], True), F32),
            jnp.asarray(np.stack(dec), F32))


def _rope_tables(l, n_heads):
    rows = l // GRID_W
    row = jnp.repeat(jnp.arange(rows, dtype=F32), GRID_W)
    col = jnp.tile(jnp.arange(GRID_W, dtype=F32), rows)
    nf = HEAD_DIM // 4
    freqs = ROPE_BASE ** (-jnp.arange(nf, dtype=F32) / nf)
    ang = jnp.concatenate([row[:, None] * freqs, col[:, None] * freqs], axis=-1)
    cos = jnp.tile(jnp.cos(ang), (1, 2 * n_heads))
    sin = jnp.tile(jnp.concatenate([-jnp.sin(ang), jnp.sin(ang)], axis=-1), (1, n_heads))
    return cos, sin


def _group_norm(y, ones, w, b, eps):
    inv_n = 1.0 / HEAD_DIM
    mu = _dot(y.astype(BF16), ones) * inv_n
    d = y - mu
    var = _dot((d * d).astype(BF16), ones) * inv_n
    return d * lax.rsqrt(var + eps) * w + b


def _out_ffn_kernel(x_ref, yf_ref, yb_ref, bonus_ref, gate_ref, of_ref, ob_ref, pg_ref,
                    lnw_ref, lnb_ref, gnw_ref, gnb_ref, ones_ref, wo_ref,
                    g1_ref, sh2_ref, sc2_ref, g2_ref, n2_ref, nf_ref, wfi_ref, wfo_ref,
                    o_ref, *, width, hidden, th, final_norm):
    ones = ones_ref[...]
    w = width
    o_rwkv = (_group_norm(yf_ref[0] + yb_ref[0], ones, lnw_ref[...], lnb_ref[...], RWKV_GN_EPS)
              + bonus_ref[0]) * gate_ref[0]
    pg = pg_ref[0]
    o_ret = (_group_norm(of_ref[0], ones, gnw_ref[...], gnb_ref[...], RET_GN_EPS) * _silu(pg[:, :w])
             + _group_norm(ob_ref[0], ones, gnw_ref[...], gnb_ref[...], RET_GN_EPS) * _silu(pg[:, w:]))
    attn = _dot(o_rwkv.astype(BF16), wo_ref[:w]) + _dot(o_ret.astype(BF16), wo_ref[w:])
    x1 = x_ref[0] + g1_ref[0] * attn
    h = _modnorm(x1, n2_ref[...], sh2_ref[0], sc2_ref[0]).astype(BF16)
    acc = None
    for c0 in range(0, hidden, th):
        gate = _dot(h, wfi_ref[:, c0:c0 + th])
        up = _dot(h, wfi_ref[:, hidden + c0:hidden + c0 + th])
        part = _dot((_silu(gate) * up).astype(BF16), wfo_ref[c0:c0 + th])
        acc = part if acc is None else acc + part
    x2 = x1 + g2_ref[0] * acc
    if final_norm:
        ms = jnp.mean(x2 * x2, axis=-1, keepdims=True)
        x2 = (x2 * lax.rsqrt(ms + NORM_EPS)) * nf_ref[...]
    o_ref[0] = x2


def _out_ffn(x, yf, yb, bonus, gate, of, ob, pg, consts, mods, final_norm, tm):
    b, l, d = x.shape
    w = yf.shape[-1]
    hidden = consts["w_ffn_out"].shape[0]
    th = hidden // 4 if (hidden // 4) % 128 == 0 else hidden
    bm = mods[0].shape[0]
    mod_map = (lambda i, j: (i, 0, 0)) if bm > 1 else (lambda i, j: (0, 0, 0))
    tok = lambda width: pl.BlockSpec((1, tm, width), lambda i, j: (i, j, 0))
    const = lambda arr: pl.BlockSpec(arr.shape, lambda i, j: (0,) * arr.ndim, pipeline_mode=pl.Buffered(1))
    names = ("ln_w", "ln_b", "gn_w", "gn_b", "ones", "w_out")
    tail = ("norm2", "norm_f", "w_ffn_in", "w_ffn_out")
    in_specs = ([tok(d)] + [tok(w)] * 6 + [tok(2 * w)] + [const(consts[n]) for n in names]
                + [pl.BlockSpec((1, 1, d), mod_map)] * 4 + [const(consts[n]) for n in tail])
    args = ([x, yf, yb, bonus, gate, of, ob, pg] + [consts[n] for n in names] + list(mods)
            + [consts[n] for n in tail])
    kern = functools.partial(_out_ffn_kernel, width=w, hidden=hidden, th=th, final_norm=final_norm)
    return pl.pallas_call(
        kern,
        grid=(b, l // tm),
        in_specs=in_specs,
        out_specs=tok(d),
        out_shape=jax.ShapeDtypeStruct((b, l, d), F32),
        compiler_params=_params("arbitrary", "arbitrary"),
        name="out_ffn",
    )(*args)


def _pad_rows(a, rows, offset=0):
    out = jnp.zeros((rows,) + a.shape[1:], a.dtype)
    return out.at[offset:offset + a.shape[0]].set(a)


def _layer_consts(i, w_in, w_vres_down, mu_rwkv, mu_vres, w0, w_up, a0, a_up, g_up, k_k, k_a, r_k, v0, v_up,
                  ln_x_w, ln_x_b, ret_gn_w, ret_gn_b, w_out, w_ffn_in, w_ffn_out, norm2, norm_f):
    width = w0.shape[-1]
    n_decay, n_iclr, n_gate = w_up.shape[2], a_up.shape[2], g_up.shape[1]
    n_vres = v_up.shape[1]
    n_rwkv = 3 * width + n_decay + n_iclr + n_gate
    assert n_decay + n_iclr == 128 and n_gate + n_vres <= 256
    rwkv_cols = 3 * width + 128 + 256
    d = w_in.shape[1]
    has_vres = i > 0
    wi = w_in[i]
    pad = jnp.zeros((d, rwkv_cols - n_rwkv - n_vres), F32)
    vres_cols = w_vres_down[i - 1] if has_vres else jnp.zeros((d, n_vres), F32)
    w_proj = jnp.concatenate([wi[:, :n_rwkv], vres_cols, pad, wi[:, n_rwkv:]], axis=1).astype(BF16)
    mu_v = mu_vres[i - 1] if has_vres else jnp.zeros((n_vres,), F32)
    mu = jnp.concatenate([mu_rwkv[i], mu_v, jnp.zeros((rwkv_cols - n_rwkv - n_vres,), F32)])[None]
    cat2 = lambda a: jnp.concatenate([a[0], a[1]], axis=-1)
    head = jnp.arange(width) // HEAD_DIM
    ones = (head[:, None] == head[None, :]).astype(BF16)
    row = lambda a: a.reshape(1, -1)
    return dict(
        w_proj=w_proj, widths=(rwkv_cols, 3 * width, 2 * width), n_rwkv=n_rwkv, n_vres=n_vres, mu=mu,
        w0=row(cat2(w0[i])), w_up=_pad_rows(cat2(w_up[i]), 128), a0=row(cat2(a0[i])),
        a_up=_pad_rows(cat2(a_up[i]), 128, n_decay), g_up=_pad_rows(g_up[i], 256),
        v_up=_pad_rows(v_up[i - 1] if has_vres else jnp.zeros((n_vres, width), F32), 256, n_gate),
        v0=row(v0[i - 1] if has_vres else jnp.zeros((width,), F32)),
        k_k=row(k_k[i]), k_a=row(k_a[i]), r_k=row(r_k[i]), ones=ones,
        ln_w=row(ln_x_w[i]), ln_b=row(ln_x_b[i]), gn_w=row(ret_gn_w[i]), gn_b=row(ret_gn_b[i]),
        w_out=w_out[i].astype(BF16), w_ffn_in=w_ffn_in[i].astype(BF16), w_ffn_out=w_ffn_out[i].astype(BF16),
        norm2=row(norm2[i]), norm_f=row(norm_f),
    )


def _mixers(x, norm1, shift, scale, consts, vfirst, s_rwkv, s_ret, rope, tables, quad, tm):
    p_rwkv, p_qkv, p_g = _in_proj(x, norm1, shift, scale, consts["w_proj"], consts["widths"], rope, tm)
    r, v, kk, bonus, gate, lwf, lwb, kf, kb, bf, bb = _rwkv_prep(
        p_rwkv, vfirst, consts, quad, min(TM_PREP, x.shape[1]) if quad else x.shape[1])
    yf, yb, s_rwkv = _rwkv_scan(r, v, kk, lwf, lwb, kf, kb, bf, bb, s_rwkv)
    of, ob, s_ret = _ret_scan(p_qkv, tables, s_ret)
    return (yf, yb, bonus, gate, of, ob, p_g), v, s_rwkv, s_ret


def kernel(x, c, ctx, c_ctx, w_ada, b_ada, norm1, norm2, norm_f, w_in, w_vres_down, mu_rwkv, mu_vres, w0, w_up, a0, a_up, g_up, k_k, k_a, r_k, v0, v_up, ln_x_w, ln_x_b, ret_gn_w, ret_gn_b, w_out, w_ffn_in, w_ffn_out):
    b, l, d = x.shape
    l_ctx = ctx.shape[1]
    depth = w_in.shape[0]
    width = w0.shape[-1]
    n_heads = width // HEAD_DIM
    n_pairs = n_heads // 2
    tm = min(TM_PROJ, l)
    tm_ctx = min(TM_PROJ, l_ctx)
    tm_ffn = min(TM_FFN, l)
    tm_ffn_ctx = min(TM_FFN, l_ctx)

    m_pad = -(-(b + 1) // 8) * 8
    cond = jnp.zeros((m_pad, d), F32).at[:b].set(c).at[b].set(c_ctx)
    mod = _ada(cond, w_ada, b_ada)

    tables = _ret_tables(n_heads)
    rope_l = _rope_tables(l, n_heads)
    rope_c = None
    zero_ret = jnp.zeros((2, b, width // PAIR, PAIR, PAIR), F32)
    zero_rwkv = jnp.zeros((2, b, width // RWKV_GROUP, RWKV_GROUP, RWKV_GROUP), F32)

    vfirst_c = vfirst_l = None
    for i in range(depth):
        last = i == depth - 1
        consts = _layer_consts(i, w_in, w_vres_down, mu_rwkv, mu_vres, w0, w_up, a0, a_up, g_up, k_k, k_a, r_k,
                               v0, v_up, ln_x_w, ln_x_b, ret_gn_w, ret_gn_b, w_out, w_ffn_in, w_ffn_out,
                               norm2, norm_f)
        mods_l = [mod[i, :b, None, j * d:(j + 1) * d] for j in range(6)]
        mods_c = [mod[i, b:b + 1, None, j * d:(j + 1) * d] for j in range(6)]
        n1 = norm1[i][None]

        mix_c, v_c, s_rwkv, s_ret = _mixers(ctx, n1, mods_c[0], mods_c[1], consts, vfirst_c,
                                            zero_rwkv, zero_ret, rope_c, tables, False, tm_ctx)
        mix_l, v_l, _, _ = _mixers(x, n1, mods_l[0], mods_l[1], consts, vfirst_l,
                                   s_rwkv, s_ret, rope_l, tables, True, tm)
        if i == 0:
            vfirst_c, vfirst_l = v_c, v_l
        x = _out_ffn(x, *mix_l, consts, (mods_l[2], mods_l[3], mods_l[4], mods_l[5]), last, tm_ffn)
        if not last:
            ctx = _out_ffn(ctx, *mix_c, consts, (mods_c[2], mods_c[3], mods_c[4], mods_c[5]), False, tm_ffn_ctx)
    return x
```

```python
import functools

import numpy as np
import jax
import jax.numpy as jnp
from jax import lax
from jax.experimental import pallas as pl
from jax.experimental.pallas import tpu as pltpu

F32 = jnp.float32
BF16 = jnp.bfloat16
ACT = BF16

GRID_W = 64
HEAD_DIM = 64
PAIR = 2 * HEAD_DIM
RWKV_GROUP = 2 * HEAD_DIM
RWKV_CHUNK = 64
RET_CHUNK = 128
ROPE_BASE = 10000.0
NORM_EPS = 1e-6
RWKV_GN_EPS = 64e-5
RET_GN_EPS = 1e-5
VMEM_LIMIT = 56 * 1024 * 1024
TM_PROJ = 512
TM_PREP = 256
TM_FFN = 512
SCAN_BATCH = 4
LORA_PASSES = 1
HEAD_SUM_PIECES = 2
RWKV_PASSES = dict(state=1, gram=1, inverse=1, kv=1, solve=1, out=1, update=1)

NN = ((1,), (0,))
NT = ((1,), (1,))


def _dot(a, b, dims=NN):
    return lax.dot_general(a, b, (dims, ((), ())), preferred_element_type=F32)


def _split(a):
    hi = a.astype(BF16)
    lo = (a - hi.astype(F32)).astype(BF16)
    return hi, lo


def _mm(a, b, dims=NN, passes=1):
    if passes == 1:
        return _dot(a.astype(BF16), b.astype(BF16), dims)
    ah, al = _split(a)
    bh, bl = _split(b)
    lhs = jnp.concatenate([ah, al, ah], axis=1)
    rhs = jnp.concatenate([bh, bh, bl], axis=0 if dims == NN else 1)
    return _dot(lhs, rhs, dims)


def _mm_exact_lhs(a_bf16, b, pieces=3):
    acc = None
    rem = b
    for _ in range(pieces):
        p = rem.astype(BF16)
        t = _dot(a_bf16, p)
        acc = t if acc is None else acc + t
        rem = rem - p.astype(F32)
    return acc


def _mm_exact_rhs(a, b_bf16, pieces=3):
    acc = None
    rem = a
    for _ in range(pieces):
        p = rem.astype(BF16)
        t = _dot(p, b_bf16)
        acc = t if acc is None else acc + t
        rem = rem - p.astype(F32)
    return acc


def _sigmoid(x):
    return 1.0 / (1.0 + jnp.exp(-x))


def _silu(x):
    return x * _sigmoid(x)


def _softplus(z):
    return jnp.maximum(z, 0.0) + jnp.log(1.0 + jnp.exp(-jnp.abs(z)))


def _params(*sem):
    return pltpu.CompilerParams(dimension_semantics=sem, vmem_limit_bytes=VMEM_LIMIT)


def _ada_kernel(c_ref, w_ref, b_ref, o_ref):
    o_ref[0] = _mm(_silu(c_ref[...]), w_ref[0], passes=3) + b_ref[0]


def _ada(cond, w_ada, b_ada):
    depth, d, n = w_ada.shape
    m = cond.shape[0]
    tn = n // 4
    return pl.pallas_call(
        _ada_kernel,
        grid=(depth, n // tn),
        in_specs=[pl.BlockSpec((m, d), lambda i, j: (0, 0)),
                  pl.BlockSpec((1, d, tn), lambda i, j: (i, 0, j)),
                  pl.BlockSpec((1, 1, tn), lambda i, j: (i, 0, j))],
        out_specs=pl.BlockSpec((1, m, tn), lambda i, j: (i, 0, j)),
        out_shape=jax.ShapeDtypeStruct((depth, m, n), F32),
        compiler_params=_params("arbitrary", "arbitrary"),
        name="ada",
    )(cond, w_ada, b_ada.reshape(depth, 1, n))


def _modnorm(x, g, shift, scale):
    ms = jnp.mean(x * x, axis=-1, keepdims=True)
    y = (x * lax.rsqrt(ms + NORM_EPS)) * g
    return y * (1.0 + scale) + shift


def _in_proj_kernel(*refs, widths, tn, rope):
    x_ref, g_ref, sh_ref, sc_ref, w_ref = refs[:5]
    cos_ref, sin_ref = refs[5:7] if rope else (None, None)
    out_refs = refs[7:] if rope else refs[5:]
    h = _modnorm(x_ref[0], g_ref[...], sh_ref[0], sc_ref[0]).astype(BF16)
    col = 0
    for idx, (o_ref, width) in enumerate(zip(out_refs, widths)):
        for c0 in range(0, width, tn):
            c1 = min(c0 + tn, width)
            y = _dot(h, w_ref[:, col + c0:col + c1])
            if idx == 1 and c1 <= 2 * (width // 3):
                if rope:
                    y = _rope(y, cos_ref[...], sin_ref[...])
                if c0 >= width // 3:
                    y = y * HEAD_DIM ** -0.5
            o_ref[0, :, c0:c1] = y.astype(o_ref.dtype)
        col += width


def _in_proj(x, g, shift, scale, w, widths, rope, tm):
    b, l, d = x.shape
    bm = shift.shape[0]
    tn = widths[1] // 3
    mod_map = (lambda i, j: (i, 0, 0)) if bm > 1 else (lambda i, j: (0, 0, 0))
    kern = functools.partial(_in_proj_kernel, widths=widths, tn=tn, rope=rope is not None)
    in_specs = [pl.BlockSpec((1, tm, d), lambda i, j: (i, j, 0)),
                pl.BlockSpec((1, d), lambda i, j: (0, 0)),
                pl.BlockSpec((1, 1, d), mod_map),
                pl.BlockSpec((1, 1, d), mod_map),
                pl.BlockSpec(w.shape, lambda i, j: (0, 0))]
    args = [x, g, shift, scale, w]
    if rope is not None:
        in_specs += [pl.BlockSpec((tm, tn), lambda i, j: (j, 0))] * 2
        args += list(rope)
    return pl.pallas_call(
        kern,
        grid=(b, l // tm),
        in_specs=in_specs,
        out_specs=[pl.BlockSpec((1, tm, wd), lambda i, j: (i, j, 0)) for wd in widths],
        out_shape=[jax.ShapeDtypeStruct((b, l, wd), ACT) for wd in widths],
        compiler_params=_params("arbitrary", "arbitrary"),
        name="in_proj",
    )(*args)


def _direction_segments(n_cols, parts, offset, width):
    q = n_cols // parts
    return [(offset + i * q, offset + (i + 1) * q, i) for i in range(parts)]


def _shift_lerp(x, mu_ref, shifted_fn, segments, n_cols):
    outs = []
    for g0 in range(0, n_cols, 128):
        xg = x[:, g0:g0 + 128]
        lane = lax.broadcasted_iota(jnp.int32, xg.shape, 1) + g0
        s = None
        for (c0, c1, d) in segments:
            lo, hi = max(c0, g0), min(c1, g0 + 128)
            if lo >= hi:
                continue
            sd = shifted_fn(xg, d, g0)
            if lo == g0 and hi == g0 + 128:
                s = sd
            else:
                part = jnp.where((lane >= lo) & (lane < hi), sd, 0.0)
                s = part if s is None else s + part
        if s is None:
            s = jnp.zeros_like(xg)
        outs.append(xg + (s - xg) * mu_ref[:, g0:g0 + 128])
    return jnp.concatenate(outs, axis=1)


def _rwkv_prep_kernel(*refs, quad, has_vres, segments, width, n_cols):
    it = iter(refs)
    cur_ref = next(it)
    prev_ref = next(it) if quad else None
    next_ref = next(it) if quad else None
    vfirst_ref = next(it) if has_vres else None
    (mu_ref, w0_ref, wup_ref, a0_ref, aup_ref, gup_ref, vup_ref, v0_ref, kk_ref, ka_ref, rk_ref,
     ones_ref) = (next(it) for _ in range(12))
    (r_o, v_o, kk_o, bonus_o, gate_o, lwf_o, lwb_o, kf_o, kb_o, bf_o, bb_o) = (next(it) for _ in range(11))

    x = cur_ref[0].astype(F32)
    t = x.shape[0]
    row = lax.broadcasted_iota(jnp.int32, (t, 128), 0)
    if quad:
        gcol = row % GRID_W
        first = pl.program_id(1) == 0
        last = pl.program_id(1) == pl.num_programs(1) - 1

        def shifted(xg, d, g0):
            if d == 0:
                return jnp.where(gcol == 0, 0.0, pltpu.roll(xg, 1, 0))
            if d == 1:
                return jnp.where(gcol == GRID_W - 1, 0.0, pltpu.roll(xg, t - 1, 0))
            if d == 2:
                halo = jnp.where(first, 0.0, prev_ref[0, :, g0:g0 + 128].astype(F32))
                return jnp.concatenate([halo, xg[:t - GRID_W]], axis=0)
            halo = jnp.where(last, 0.0, next_ref[0, :, g0:g0 + 128].astype(F32))
            return jnp.concatenate([xg[GRID_W:], halo], axis=0)
    else:
        def shifted(xg, d, g0):
            if d == 0:
                return jnp.where(row == 0, 0.0, pltpu.roll(xg, 1, 0))
            return jnp.where(row == t - 1, 0.0, pltpu.roll(xg, t - 1, 0))

    u = _shift_lerp(x, mu_ref, shifted, segments, n_cols)
    w = width
    r, k, v = u[:, :w], u[:, w:2 * w], u[:, 2 * w:3 * w]
    lo_wa = u[:, 3 * w:3 * w + 128]
    lo_gv = u[:, 3 * w + 128:3 * w + 384]

    lp = LORA_PASSES
    logw = -jnp.exp(-_softplus(-(w0_ref[...] + _mm(jnp.tanh(lo_wa), wup_ref[...], passes=lp))) - 0.5)
    iclr = _sigmoid(a0_ref[...] + _mm(lo_wa, aup_ref[...], passes=lp))
    gate = _mm(_sigmoid(lo_gv), gup_ref[...], passes=lp)
    if has_vres:
        mix = _sigmoid(v0_ref[...] + _mm(lo_gv, vup_ref[...], passes=lp))
        v = v + (vfirst_ref[0].astype(F32) - v) * mix

    ones = ones_ref[...]
    kk = k * kk_ref[...]
    kk = kk * lax.rsqrt(jnp.maximum(_mm_exact_rhs(kk * kk, ones, HEAD_SUM_PIECES), 1e-24))
    bonus = _mm_exact_rhs(r * k * rk_ref[...], ones, HEAD_SUM_PIECES) * v

    for o_ref, val in ((r_o, r), (v_o, v), (kk_o, kk), (bonus_o, bonus), (gate_o, gate)):
        o_ref[0] = val.astype(o_ref.dtype)
    ka = ka_ref[...]
    for d, (lw_o, k_o, b_o) in enumerate(((lwf_o, kf_o, bf_o), (lwb_o, kb_o, bb_o))):
        a = iclr[:, d * w:(d + 1) * w]
        lw_o[0] = logw[:, d * w:(d + 1) * w]
        k_o[0] = (k * (1.0 + (a - 1.0) * ka)).astype(k_o.dtype)
        b_o[0] = (kk * a).astype(b_o.dtype)


def _rwkv_prep(p, vfirst, consts, quad, tm):
    b, l, n_cols = p.shape
    w = consts["k_k"].shape[-1]
    has_vres = vfirst is not None
    n_rwkv = consts["n_rwkv"]
    n_vres = consts["n_vres"]
    parts = 4 if quad else 2
    segments = _direction_segments(n_rwkv, parts, 0, w) + _direction_segments(n_vres, parts, n_rwkv, w)
    kern = functools.partial(_rwkv_prep_kernel, quad=quad, has_vres=has_vres, segments=segments,
                             width=w, n_cols=n_cols)
    rows = tm // GRID_W
    n_rows = l // GRID_W
    in_specs = [pl.BlockSpec((1, tm, n_cols), lambda i, j: (i, j, 0))]
    args = [p]
    if quad:
        in_specs += [
            pl.BlockSpec((1, GRID_W, n_cols), lambda i, j: (i, jnp.maximum(j * rows - 1, 0), 0)),
            pl.BlockSpec((1, GRID_W, n_cols), lambda i, j: (i, jnp.minimum((j + 1) * rows, n_rows - 1), 0)),
        ]
        args += [p, p]
    if has_vres:
        in_specs.append(pl.BlockSpec((1, tm, w), lambda i, j: (i, j, 0)))
        args.append(vfirst)
    for name in ("mu", "w0", "w_up", "a0", "a_up", "g_up", "v_up", "v0", "k_k", "k_a", "r_k", "ones"):
        arr = consts[name]
        in_specs.append(pl.BlockSpec(arr.shape, lambda i, j: (0, 0)))
        args.append(arr)
    return pl.pallas_call(
        kern,
        grid=(b, l // tm),
        in_specs=in_specs,
        out_specs=[pl.BlockSpec((1, tm, w), lambda i, j: (i, j, 0))] * 11,
        out_shape=[jax.ShapeDtypeStruct((b, l, w), ACT)] * 5 + [jax.ShapeDtypeStruct((b, l, w), F32)] * 2
        + [jax.ShapeDtypeStruct((b, l, w), ACT)] * 4,
        compiler_params=_params("arbitrary", "arbitrary"),
        name="rwkv_prep",
    )(*args)


def _block_diag(y, block=HEAD_DIM):
    blk = lax.broadcasted_iota(jnp.int32, y.shape, 1) // block
    return jnp.concatenate([jnp.where(blk == h, y, 0.0) for h in range(y.shape[1] // block)], axis=0)


def _head_diag_mask(width):
    r = lax.broadcasted_iota(jnp.int32, (width, width), 0) // HEAD_DIM
    c = lax.broadcasted_iota(jnp.int32, (width, width), 1) // HEAD_DIM
    return r == c


def _rwkv_stream(r, v, kk, lw, kd, bd, reverse, state, out):
    r, v, kk, kd, bd = (t.astype(F32) for t in (r, v, kk, kd, bd))
    c = r.shape[0]
    t_idx = lax.broadcasted_iota(jnp.int32, (c, c), 0)
    s_idx = lax.broadcasted_iota(jnp.int32, (c, c), 1)
    tri = jnp.where((s_idx >= t_idx) if reverse else (s_idx <= t_idx), 1.0, 0.0).astype(BF16)
    lc = _mm_exact_lhs(tri, lw)
    l_end = lc[0:1] if reverse else lc[c - 1:c]
    e_inv = jnp.exp(-lc)
    to_end = jnp.exp(l_end - lc)
    heads = RWKV_GROUP // HEAD_DIM
    row = lax.broadcasted_iota(jnp.int32, (c, heads * c), 0)
    col = lax.broadcasted_iota(jnp.int32, (c, heads * c), 1) % c
    return dict(
        state=state, out=out,
        v=v, r_t=r * jnp.exp(lc), a_t=-kk * jnp.exp(lc - lw), k_t=kd * e_inv, b_t=bd * e_inv,
        k_hat=kd * to_end, b_hat=bd * to_end, p_end=jnp.exp(l_end),
        before=(col > row) if reverse else (col < row),
        upto=(col >= row) if reverse else (col <= row),
        eye=jnp.where(col == row, 1.0, 0.0))


def _rwkv_step(streams, s_scr, passes):
    c = streams[0]["v"].shape[0]
    gw = RWKV_GROUP
    chains = [(k, p) for k in range(len(streams)) for p in range(streams[0]["v"].shape[1] // gw)]
    sl = lambda p: slice(p * gw, (p + 1) * gw)
    diag = _head_diag_mask(gw)

    z, zs, s0 = {}, {}, {}
    for ch in chains:
        st, p = streams[ch[0]], ch[1]
        z[ch] = jnp.concatenate([st["a_t"][:, sl(p)], st["r_t"][:, sl(p)]], axis=0)
        s0[ch] = s_scr[st["state"] + (p,)]
        zs[ch] = _mm(z[ch], s0[ch], NT, passes["state"])
    l_ab, m_rb, l_ak, m_rk = {}, {}, {}, {}
    for ch in chains:
        st, p = streams[ch[0]], ch[1]
        w_bk = jnp.concatenate([_block_diag(st["b_t"][:, sl(p)]), _block_diag(st["k_t"][:, sl(p)])], axis=0)
        gram = _mm(z[ch], w_bk, NT, passes["gram"])
        nb = gram.shape[1] // 2
        l_ab[ch] = jnp.where(st["before"], gram[:c, :nb], 0.0)
        m_rb[ch] = jnp.where(st["upto"], gram[c:, :nb], 0.0)
        l_ak[ch] = jnp.where(st["before"], gram[:c, nb:], 0.0)
        m_rk[ch] = jnp.where(st["upto"], gram[c:, nb:], 0.0)

    inv = {ch: streams[ch[0]]["eye"] + l_ab[ch] for ch in chains}
    power = {ch: _mm(l_ab[ch], _block_diag(l_ab[ch], c), NN, passes["inverse"]) for ch in chains}
    kv = {ch: _mm(jnp.concatenate([l_ak[ch], m_rk[ch]], axis=0),
                  _block_diag(streams[ch[0]]["v"][:, sl(ch[1])]), NN, passes["kv"]) for ch in chains}
    for _ in range(int(np.log2(c)) - 2):
        for ch in chains:
            w_ip = jnp.concatenate([_block_diag(inv[ch], c), _block_diag(power[ch], c)], axis=1)
            prod = _mm(power[ch], w_ip, NN, passes["inverse"])
            nb = prod.shape[1] // 2
            inv[ch] = inv[ch] + prod[:, :nb]
            power[ch] = prod[:, nb:]
    for ch in chains:
        inv[ch] = inv[ch] + _mm(power[ch], _block_diag(inv[ch], c), NN, passes["inverse"])

    u = {ch: _mm(inv[ch], _block_diag(zs[ch][:c] + kv[ch][:c]), NN, passes["solve"]) for ch in chains}
    for ch in chains:
        st, p = streams[ch[0]], ch[1]
        y_ref, g = st["out"]
        y_ref[g, :, sl(p)] = zs[ch][c:] + kv[ch][c:] + _mm(m_rb[ch], _block_diag(u[ch]), NN, passes["out"])
    for ch in chains:
        st, p = streams[ch[0]], ch[1]
        uv_t =jnp.concatenate([u[ch], st["v"][:, sl(p)]], axis=0).T
        upd = _mm(uv_t, jnp.concatenate([st["b_hat"][:, sl(p)], st["k_hat"][:, sl(p)]], axis=0), NN,
                  passes["update"])
        s_scr[st["state"] + (p,)] = s0[ch] * st["p_end"][:, sl(p)] + jnp.where(diag, upd, 0.0)


def _rwkv_scan_kernel(rf, vf, kkf, lwf, kf, bf, rb, vb, kkb, lwb, kb, bb, s0_ref,
                      yf_ref, yb_ref, sout_ref, s_scr, *, passes):
    i = pl.program_id(1)

    @pl.when(i == 0)
    def _():
        s_scr[...] = s0_ref[...]

    streams = []
    for g in range(rf.shape[0]):
        streams.append(_rwkv_stream(rf[g], vf[g], kkf[g], lwf[g], kf[g], bf[g], False, (0, g), (yf_ref, g)))
        streams.append(_rwkv_stream(rb[g], vb[g], kkb[g], lwb[g], kb[g], bb[g], True, (1, g), (yb_ref, g)))
    _rwkv_step(streams, s_scr, passes)

    @pl.when(i == pl.num_programs(1) - 1)
    def _():
        sout_ref[...] = s_scr[...]


def _rwkv_scan(r, v, kk, lwf, lwb, kf, kb, bf, bb, s0):
    b, l, w = r.shape
    c = RWKV_CHUNK
    nc = l // c
    g = SCAN_BATCH if b % SCAN_BATCH == 0 else 1
    fwd = pl.BlockSpec((g, c, w), lambda i, j: (i, j, 0))
    bwd = pl.BlockSpec((g, c, w), lambda i, j: (i, nc - 1 - j, 0))
    st = pl.BlockSpec((2, g) + s0.shape[2:], lambda i, j: (0, i, 0, 0, 0))
    return pl.pallas_call(
        functools.partial(_rwkv_scan_kernel, passes=dict(RWKV_PASSES)),
        grid=(b // g, nc),
        in_specs=[fwd] * 6 + [bwd] * 6 + [st],
        out_specs=[fwd, bwd, st],
        out_shape=[jax.ShapeDtypeStruct((b, l, w), F32)] * 2 + [jax.ShapeDtypeStruct(s0.shape, F32)],
        scratch_shapes=[pltpu.VMEM((2, g) + s0.shape[2:], F32)],
        compiler_params=_params("arbitrary", "arbitrary"),
        name="rwkv_scan",
    )(r, v, kk, lwf, kf, bf, r, v, kk, lwb, kb, bb, s0)


def _rope(x, cos, sin_signed):
    w = x.shape[1]
    half = HEAD_DIM // 2
    lane = lax.broadcasted_iota(jnp.int32, x.shape, 1) % HEAD_DIM
    rot = jnp.where(lane < half, pltpu.roll(x, w - half, 1), pltpu.roll(x, half, 1))
    return x * cos + rot * sin_signed


def _ret_step(streams, s_scr):
    n_pairs = streams[0]["v"].shape[1] // PAIR
    chains = [(k, p) for k in range(len(streams)) for p in range(n_pairs)]
    sl = lambda p: slice(p * PAIR, (p + 1) * PAIR)
    diag = _head_diag_mask(PAIR)
    scores, cross, s0 = {}, {}, {}
    for ch in chains:
        st, p = streams[ch[0]], ch[1]
        s0[ch] = s_scr[st["state"] + (p,)]
        scores[ch] = _mm(st["q"][:, sl(p)], _block_diag(st["k"][:, sl(p)]), NT) * st["inner"][p]
        cross[ch] = _mm(st["q"][:, sl(p)], s0[ch]) * st["qdec"][:, sl(p)]
    for ch in chains:
        st, p = streams[ch[0]], ch[1]
        o_ref, g = st["out"]
        o_ref[g, :, sl(p)] = _mm(scores[ch], _block_diag(st["v"][:, sl(p)])) + cross[ch]
    for ch in chains:
        st, p = streams[ch[0]], ch[1]
        upd = _mm((st["k"][:, sl(p)] * st["kdec"][:, sl(p)]).T, st["v"][:, sl(p)])
        s_scr[st["state"] + (p,)] = s0[ch] * st["cdec"][:, sl(p)] + jnp.where(diag, upd, 0.0)


def _ret_scan_kernel(qkvf, qkvb, inf_ref, inb_ref, dec_ref, s0_ref,
                     of_ref, ob_ref, sout_ref, s_scr, *, width):
    i = pl.program_id(1)

    @pl.when(i == 0)
    def _():
        s_scr[...] = s0_ref[...]

    w = width
    streams = []
    for g in range(qkvf.shape[0]):
        for d, (qkv, inner_ref, o_ref) in enumerate(((qkvf, inf_ref, of_ref), (qkvb, inb_ref, ob_ref))):
            x = qkv[g]
            streams.append(dict(
                q=x[:, :w], k=x[:, w:2 * w], v=x[:, 2 * w:], inner=inner_ref, qdec=dec_ref[d, 0],
                kdec=dec_ref[d, 1], cdec=dec_ref[d, 2, 0:1], state=(d, g), out=(o_ref, g)))
    _ret_step(streams, s_scr)

    @pl.when(i == pl.num_programs(1) - 1)
    def _():
        sout_ref[...] = s_scr[...]


def _ret_scan(qkv, tables, s0):
    b, l, w3 = qkv.shape
    w = w3 // 3
    c = RET_CHUNK
    nc = l // c
    n_pairs = w // PAIR
    g = SCAN_BATCH if b % SCAN_BATCH == 0 else 1
    inner_f, inner_b, dec = tables
    st = pl.BlockSpec((2, g, n_pairs, PAIR, PAIR), lambda i, j: (0, i, 0, 0, 0))
    const3 = pl.BlockSpec(inner_f.shape, lambda i, j: (0, 0, 0))
    return pl.pallas_call(
        functools.partial(_ret_scan_kernel, width=w),
        grid=(b // g, nc),
        in_specs=[pl.BlockSpec((g, c, w3), lambda i, j: (i, j, 0)),
                  pl.BlockSpec((g, c, w3), lambda i, j: (i, nc - 1 - j, 0)),
                  const3, const3,
                  pl.BlockSpec(dec.shape, lambda i, j: (0, 0, 0, 0)),
                  st],
        out_specs=[pl.BlockSpec((g, c, w), lambda i, j: (i, j, 0)),
                   pl.BlockSpec((g, c, w), lambda i, j: (i, nc - 1 - j, 0)),
                   st],
        out_shape=[jax.ShapeDtypeStruct((b, l, w), F32)] * 2 + [jax.ShapeDtypeStruct(s0.shape, F32)],
        scratch_shapes=[pltpu.VMEM((2, g, n_pairs, PAIR, PAIR), F32)],
        compiler_params=_params("arbitrary", "arbitrary"),
        name="ret_scan",
    )(qkv, qkv, inner_f, inner_b, dec, s0)


def _ret_tables(n_heads):
    c = RET_CHUNK
    lg = np.log(1.0 - 2.0 ** (-5.0 - np.arange(n_heads, dtype=np.float64)))
    idx = np.arange(c, dtype=np.float64)
    diff = idx[:, None] - idx[None, :]

    def inner(log_gamma, reverse):
        dd = -diff if reverse else diff
        m = np.where(dd >= 0, np.exp(log_gamma[:, None, None] * np.maximum(dd, 0.0)), 0.0)
        return m.reshape(n_heads // 2, 2, c, c).transpose(0, 2, 1, 3).reshape(n_heads // 2, c, 2 * c)

    def per_lane(tab):
        return np.repeat(tab, HEAD_DIM, axis=1)

    dec = []
    for reverse in (False, True):
        g = lg[::-1] if reverse else lg
        pos = (c - 1.0 - idx) if reverse else idx
        qd = per_lane(np.exp(g[None, :] * (pos[:, None] + 1.0)))
        kd = per_lane(np.exp(g[None, :] * (c - 1.0 - pos[:, None])))
        cd = per_lane(np.broadcast_to(np.exp(g * c)[None, :], (c, n_heads)))
        dec.append(np.stack([qd, kd, cd]))
    return (jnp.asarray(inner(lg, False), F32), jnp.asarray(inner(lg[::-1], True), F32),
            jnp.asarray(np.stack(dec), F32))


def _rope_tables(l, n_heads):
    rows = l // GRID_W
    row = jnp.repeat(jnp.arange(rows, dtype=F32), GRID_W)
    col = jnp.tile(jnp.arange(GRID_W, dtype=F32), rows)
    nf = HEAD_DIM // 4
    freqs = ROPE_BASE ** (-jnp.arange(nf, dtype=F32) / nf)
    ang = jnp.concatenate([row[:, None] * freqs, col[:, None] * freqs], axis=-1)
    cos = jnp.tile(jnp.cos(ang), (1, 2 * n_heads))
    sin = jnp.tile(jnp.concatenate([-jnp.sin(ang), jnp.sin(ang)], axis=-1), (1, n_heads))
    return cos, sin


def _group_norm(y, ones, w, b, eps):
    inv_n = 1.0 / HEAD_DIM
    mu = _dot(y.astype(BF16), ones) * inv_n
    d = y - mu
    var = _dot((d * d).astype(BF16), ones) * inv_n
    return d * lax.rsqrt(var + eps) * w + b


def _out_ffn_kernel(x_ref, yf_ref, yb_ref, bonus_ref, gate_ref, of_ref, ob_ref, pg_ref,
                    lnw_ref, lnb_ref, gnw_ref, gnb_ref, ones_ref, wo_ref,
                    g1_ref, sh2_ref, sc2_ref, g2_ref, n2_ref, nf_ref, wfi_ref, wfo_ref,
                    o_ref, *, width, hidden, th, final_norm):
    ones = ones_ref[...]
    w = width
    o_rwkv = (_group_norm(yf_ref[0] + yb_ref[0], ones, lnw_ref[...], lnb_ref[...], RWKV_GN_EPS)
              + bonus_ref[0]) * gate_ref[0]
    pg = pg_ref[0].astype(F32)
    o_ret = (_group_norm(of_ref[0], ones, gnw_ref[...], gnb_ref[...], RET_GN_EPS) * _silu(pg[:, :w])
             + _group_norm(ob_ref[0], ones, gnw_ref[...], gnb_ref[...], RET_GN_EPS) * _silu(pg[:, w:]))
    attn = _dot(o_rwkv.astype(BF16), wo_ref[:w]) + _dot(o_ret.astype(BF16), wo_ref[w:])
    x1 = x_ref[0] + g1_ref[0] * attn
    h = _modnorm(x1, n2_ref[...], sh2_ref[0], sc2_ref[0]).astype(BF16)
    acc = None
    for c0 in range(0, hidden, th):
        gate = _dot(h, wfi_ref[:, c0:c0 + th])
        up = _dot(h, wfi_ref[:, hidden + c0:hidden + c0 + th])
        part = _dot((_silu(gate) * up).astype(BF16), wfo_ref[c0:c0 + th])
        acc = part if acc is None else acc + part
    x2 = x1 + g2_ref[0] * acc
    if final_norm:
        ms = jnp.mean(x2 * x2, axis=-1, keepdims=True)
        x2 = (x2 * lax.rsqrt(ms + NORM_EPS)) * nf_ref[...]
    o_ref[0] = x2


def _out_ffn(x, yf, yb, bonus, gate, of, ob, pg, consts, mods, final_norm, tm):
    b, l, d = x.shape
    w = yf.shape[-1]
    hidden = consts["w_ffn_out"].shape[0]
    th = hidden // 4 if (hidden // 4) % 128 == 0 else hidden
    bm = mods[0].shape[0]
    mod_map = (lambda i, j: (i, 0, 0)) if bm > 1 else (lambda i, j: (0, 0, 0))
    tok = lambda width: pl.BlockSpec((1, tm, width), lambda i, j: (i, j, 0))
    const = lambda arr: pl.BlockSpec(arr.shape, lambda i, j: (0,) * arr.ndim, pipeline_mode=pl.Buffered(1))
    names = ("ln_w", "ln_b", "gn_w", "gn_b", "ones", "w_out")
    tail = ("norm2", "norm_f", "w_ffn_in", "w_ffn_out")
    in_specs = ([tok(d)] + [tok(w)] * 6 + [tok(2 * w)] + [const(consts[n]) for n in names]
                + [pl.BlockSpec((1, 1, d), mod_map)] * 4 + [const(consts[n]) for n in tail])
    args = ([x, yf, yb, bonus, gate, of, ob, pg] + [consts[n] for n in names] + list(mods)
            + [consts[n] for n in tail])
    kern = functools.partial(_out_ffn_kernel, width=w, hidden=hidden, th=th, final_norm=final_norm)
    return pl.pallas_call(
        kern,
        grid=(b, l // tm),
        in_specs=in_specs,
        out_specs=tok(d),
        out_shape=jax.ShapeDtypeStruct((b, l, d), F32),
        compiler_params=_params("arbitrary", "arbitrary"),
        name="out_ffn",
    )(*args)


def _pad_rows(a, rows, offset=0):
    out = jnp.zeros((rows,) + a.shape[1:], a.dtype)
    return out.at[offset:offset + a.shape[0]].set(a)


def _layer_consts(i, w_in, w_vres_down, mu_rwkv, mu_vres, w0, w_up, a0, a_up, g_up, k_k, k_a, r_k, v0, v_up,
                  ln_x_w, ln_x_b, ret_gn_w, ret_gn_b, w_out, w_ffn_in, w_ffn_out, norm2, norm_f):
    width = w0.shape[-1]
    n_decay, n_iclr, n_gate = w_up.shape[2], a_up.shape[2], g_up.shape[1]
    n_vres = v_up.shape[1]
    n_rwkv = 3 * width + n_decay + n_iclr + n_gate
    assert n_decay + n_iclr == 128 and n_gate + n_vres <= 256
    rwkv_cols = 3 * width + 128 + 256
    d = w_in.shape[1]
    has_vres = i > 0
    wi = w_in[i]
    pad = jnp.zeros((d, rwkv_cols - n_rwkv - n_vres), F32)
    vres_cols = w_vres_down[i - 1] if has_vres else jnp.zeros((d, n_vres), F32)
    w_proj = jnp.concatenate([wi[:, :n_rwkv], vres_cols, pad, wi[:, n_rwkv:]], axis=1).astype(BF16)
    mu_v = mu_vres[i - 1] if has_vres else jnp.zeros((n_vres,), F32)
    mu = jnp.concatenate([mu_rwkv[i], mu_v, jnp.zeros((rwkv_cols - n_rwkv - n_vres,), F32)])[None]
    cat2 = lambda a: jnp.concatenate([a[0], a[1]], axis=-1)
    head = jnp.arange(width) // HEAD_DIM
    ones = (head[:, None] == head[None, :]).astype(BF16)
    row = lambda a: a.reshape(1, -1)
    return dict(
        w_proj=w_proj, widths=(rwkv_cols, 3 * width, 2 * width), n_rwkv=n_rwkv, n_vres=n_vres, mu=mu,
        w0=row(cat2(w0[i])), w_up=_pad_rows(cat2(w_up[i]), 128), a0=row(cat2(a0[i])),
        a_up=_pad_rows(cat2(a_up[i]), 128, n_decay), g_up=_pad_rows(g_up[i], 256),
        v_up=_pad_rows(v_up[i - 1] if has_vres else jnp.zeros((n_vres, width), F32), 256, n_gate),
        v0=row(v0[i - 1] if has_vres else jnp.zeros((width,), F32)),
        k_k=row(k_k[i]), k_a=row(k_a[i]), r_k=row(r_k[i]), ones=ones,
        ln_w=row(ln_x_w[i]), ln_b=row(ln_x_b[i]), gn_w=row(ret_gn_w[i]), gn_b=row(ret_gn_b[i]),
        w_out=w_out[i].astype(BF16), w_ffn_in=w_ffn_in[i].astype(BF16), w_ffn_out=w_ffn_out[i].astype(BF16),
        norm2=row(norm2[i]), norm_f=row(norm_f),
    )


def _mixers(x, norm1, shift, scale, consts, vfirst, s_rwkv, s_ret, rope, tables, quad, tm):
    p_rwkv, p_qkv, p_g = _in_proj(x, norm1, shift, scale, consts["w_proj"], consts["widths"], rope, tm)
    r, v, kk, bonus, gate, lwf, lwb, kf, kb, bf, bb = _rwkv_prep(
        p_rwkv, vfirst, consts, quad, min(TM_PREP, x.shape[1]) if quad else x.shape[1])
    yf, yb, s_rwkv = _rwkv_scan(r, v, kk, lwf, lwb, kf, kb, bf, bb, s_rwkv)
    of, ob, s_ret = _ret_scan(p_qkv, tables, s_ret)
    return (yf, yb, bonus, gate, of, ob, p_g), v, s_rwkv, s_ret


def kernel(x, c, ctx, c_ctx, w_ada, b_ada, norm1, norm2, norm_f, w_in, w_vres_down, mu_rwkv, mu_vres, w0, w_up, a0, a_up, g_up, k_k, k_a, r_k, v0, v_up, ln_x_w, ln_x_b, ret_gn_w, ret_gn_b, w_out, w_ffn_in, w_ffn_out):
    b, l, d = x.shape
    l_ctx = ctx.shape[1]
    depth = w_in.shape[0]
    width = w0.shape[-1]
    n_heads = width // HEAD_DIM
    n_pairs = n_heads // 2
    tm = min(TM_PROJ, l)
    tm_ctx = min(TM_PROJ, l_ctx)
    tm_ffn = min(TM_FFN, l)
    tm_ffn_ctx = min(TM_FFN, l_ctx)

    m_pad = -(-(b + 1) // 8) * 8
    cond = jnp.zeros((m_pad, d), F32).at[:b].set(c).at[b].set(c_ctx)
    mod = _ada(cond, w_ada, b_ada)

    tables = _ret_tables(n_heads)
    rope_l = _rope_tables(l, n_heads)
    rope_c = None
    zero_ret = jnp.zeros((2, b, width // PAIR, PAIR, PAIR), F32)
    zero_rwkv = jnp.zeros((2, b, width // RWKV_GROUP, RWKV_GROUP, RWKV_GROUP), F32)

    vfirst_c = vfirst_l = None
    for i in range(depth):
        last = i == depth - 1
        consts = _layer_consts(i, w_in, w_vres_down, mu_rwkv, mu_vres, w0, w_up, a0, a_up, g_up, k_k, k_a, r_k,
                               v0, v_up, ln_x_w, ln_x_b, ret_gn_w, ret_gn_b, w_out, w_ffn_in, w_ffn_out,
                               norm2, norm_f)
        mods_l = [mod[i, :b, None, j * d:(j + 1) * d] for j in range(6)]
        mods_c = [mod[i, b:b + 1, None, j * d:(j + 1) * d] for j in range(6)]
        n1 = norm1[i][None]

        mix_c, v_c, s_rwkv, s_ret = _mixers(ctx, n1, mods_c[0], mods_c[1], consts, vfirst_c,
                                            zero_rwkv, zero_ret, rope_c, tables, False, tm_ctx)
        mix_l, v_l, _, _ = _mixers(x, n1, mods_l[0], mods_l[1], consts, vfirst_l,
                                   s_rwkv, s_ret, rope_l, tables, True, tm)
        if i == 0:
            vfirst_c, vfirst_l = v_c, v_l
        x = _out_ffn(x, *mix_l, consts, (mods_l[2], mods_l[3], mods_l[4], mods_l[5]), last, tm_ffn)
        if not last:
            ctx = _out_ffn(ctx, *mix_c, consts, (mods_c[2], mods_c[3], mods_c[4], mods_c[5]), False, tm_ffn_ctx)
    return x
```

```python
import functools

import numpy as np
import jax
import jax.numpy as jnp
from jax import lax
from jax.experimental import pallas as pl
from jax.experimental.pallas import tpu as pltpu

F32 = jnp.float32
BF16 = jnp.bfloat16
ACT = BF16

GRID_W = 64
HEAD_DIM = 64
PAIR = 2 * HEAD_DIM
RWKV_GROUP = 2 * HEAD_DIM
RWKV_CHUNK = 64
RET_CHUNK = 128
ROPE_BASE = 10000.0
NORM_EPS = 1e-6
RWKV_GN_EPS = 64e-5
RET_GN_EPS = 1e-5
VMEM_LIMIT = 56 * 1024 * 1024
TM_PROJ = 512
TM_PREP = 256
TM_FFN = 512
MXU_TILE = 256
FFN_CHUNK_TILES = 3
SCAN_BATCH = 4
LORA_PASSES = 1
HEAD_SUM_PIECES = 2
RWKV_PASSES = dict(state=1, gram=1, inverse=1, kv=1, solve=1, out=1, update=1)

NN = ((1,), (0,))
NT = ((1,), (1,))


def _dot(a, b, dims=NN):
    return lax.dot_general(a, b, (dims, ((), ())), preferred_element_type=F32)


def _split(a):
    hi = a.astype(BF16)
    lo = (a - hi.astype(F32)).astype(BF16)
    return hi, lo


def _mm(a, b, dims=NN, passes=1):
    if passes == 1:
        return _dot(a.astype(BF16), b.astype(BF16), dims)
    ah, al = _split(a)
    bh, bl = _split(b)
    lhs = jnp.concatenate([ah, al, ah], axis=1)
    rhs = jnp.concatenate([bh, bh, bl], axis=0 if dims == NN else 1)
    return _dot(lhs, rhs, dims)


def _mm_exact_lhs(a_bf16, b, pieces=3):
    acc = None
    rem = b
    for _ in range(pieces):
        p = rem.astype(BF16)
        t = _dot(a_bf16, p)
        acc = t if acc is None else acc + t
        rem = rem - p.astype(F32)
    return acc


def _mm_exact_rhs(a, b_bf16, pieces=3):
    acc = None
    rem = a
    for _ in range(pieces):
        p = rem.astype(BF16)
        t = _dot(p, b_bf16)
        acc = t if acc is None else acc + t
        rem = rem - p.astype(F32)
    return acc


def _sigmoid(x):
    return 1.0 / (1.0 + jnp.exp(-x))


def _silu(x):
    return x * _sigmoid(x)


def _softplus(z):
    return jnp.maximum(z, 0.0) + jnp.log(1.0 + jnp.exp(-jnp.abs(z)))


def _params(*sem):
    return pltpu.CompilerParams(dimension_semantics=sem, vmem_limit_bytes=VMEM_LIMIT)


def _ada_kernel(c_ref, w_ref, b_ref, o_ref):
    o_ref[0] = _mm(_silu(c_ref[...]), w_ref[0], passes=3) + b_ref[0]


def _ada(cond, w_ada, b_ada):
    depth, d, n = w_ada.shape
    m = cond.shape[0]
    tn = n // 4
    return pl.pallas_call(
        _ada_kernel,
        grid=(depth, n // tn),
        in_specs=[pl.BlockSpec((m, d), lambda i, j: (0, 0)),
                  pl.BlockSpec((1, d, tn), lambda i, j: (i, 0, j)),
                  pl.BlockSpec((1, 1, tn), lambda i, j: (i, 0, j))],
        out_specs=pl.BlockSpec((1, m, tn), lambda i, j: (i, 0, j)),
        out_shape=jax.ShapeDtypeStruct((depth, m, n), F32),
        compiler_params=_params("arbitrary", "arbitrary"),
        name="ada",
    )(cond, w_ada, b_ada.reshape(depth, 1, n))


def _modnorm(x, g, shift, scale):
    ms = jnp.mean(x * x, axis=-1, keepdims=True)
    y = (x * lax.rsqrt(ms + NORM_EPS)) * g
    return y * (1.0 + scale) + shift


def _in_proj_kernel(*refs, widths, tn, rope):
    x_ref, g_ref, sh_ref, sc_ref, w_ref = refs[:5]
    cos_ref, sin_ref = refs[5:7] if rope else (None, None)
    out_refs = refs[7:] if rope else refs[5:]
    h = _modnorm(x_ref[0], g_ref[...], sh_ref[0], sc_ref[0]).astype(BF16)
    col = 0
    for idx, (o_ref, width) in enumerate(zip(out_refs, widths)):
        for c0 in range(0, width, tn):
            c1 = min(c0 + tn, width)
            y = _dot(h, w_ref[:, col + c0:col + c1])
            if idx == 1 and c1 <= 2 * (width // 3):
                if rope:
                    y = _rope(y, cos_ref[...], sin_ref[...])
                if c0 >= width // 3:
                    y = y * HEAD_DIM ** -0.5
            o_ref[0, :, c0:c1] = y.astype(o_ref.dtype)
        col += width


def _in_proj(x, g, shift, scale, w, widths, rope, tm):
    b, l, d = x.shape
    bm = shift.shape[0]
    tn = widths[1] // 3
    mod_map = (lambda i, j: (i, 0, 0)) if bm > 1 else (lambda i, j: (0, 0, 0))
    kern = functools.partial(_in_proj_kernel, widths=widths, tn=tn, rope=rope is not None)
    in_specs = [pl.BlockSpec((1, tm, d), lambda i, j: (i, j, 0)),
                pl.BlockSpec((1, d), lambda i, j: (0, 0)),
                pl.BlockSpec((1, 1, d), mod_map),
                pl.BlockSpec((1, 1, d), mod_map),
                pl.BlockSpec(w.shape, lambda i, j: (0, 0))]
    args = [x, g, shift, scale, w]
    if rope is not None:
        in_specs += [pl.BlockSpec((tm, tn), lambda i, j: (j, 0))] * 2
        args += list(rope)
    return pl.pallas_call(
        kern,
        grid=(b, l // tm),
        in_specs=in_specs,
        out_specs=[pl.BlockSpec((1, tm, wd), lambda i, j: (i, j, 0)) for wd in widths],
        out_shape=[jax.ShapeDtypeStruct((b, l, wd), ACT) for wd in widths],
        compiler_params=_params("arbitrary", "arbitrary"),
        name="in_proj",
    )(*args)


def _direction_segments(n_cols, parts, offset, width):
    q = n_cols // parts
    return [(offset + i * q, offset + (i + 1) * q, i) for i in range(parts)]


def _shift_lerp(x, mu_ref, shifted_fn, segments, n_cols):
    outs = []
    for g0 in range(0, n_cols, 128):
        xg = x[:, g0:g0 + 128]
        lane = lax.broadcasted_iota(jnp.int32, xg.shape, 1) + g0
        s = None
        for (c0, c1, d) in segments:
            lo, hi = max(c0, g0), min(c1, g0 + 128)
            if lo >= hi:
                continue
            sd = shifted_fn(xg, d, g0)
            if lo == g0 and hi == g0 + 128:
                s = sd
            else:
                part = jnp.where((lane >= lo) & (lane < hi), sd, 0.0)
                s = part if s is None else s + part
        if s is None:
            s = jnp.zeros_like(xg)
        outs.append(xg + (s - xg) * mu_ref[:, g0:g0 + 128])
    return jnp.concatenate(outs, axis=1)


def _rwkv_prep_kernel(*refs, quad, has_vres, segments, width, n_cols):
    it = iter(refs)
    cur_ref = next(it)
    prev_ref = next(it) if quad else None
    next_ref = next(it) if quad else None
    vfirst_ref = next(it) if has_vres else None
    (mu_ref, w0_ref, wup_ref, a0_ref, aup_ref, gup_ref, vup_ref, v0_ref, kk_ref, ka_ref, rk_ref,
     ones_ref) = (next(it) for _ in range(12))
    (r_o, v_o, kk_o, bonus_o, gate_o, lwf_o, lwb_o, kf_o, kb_o, bf_o, bb_o) = (next(it) for _ in range(11))

    x = cur_ref[0].astype(F32)
    t = x.shape[0]
    row = lax.broadcasted_iota(jnp.int32, (t, 128), 0)
    if quad:
        gcol = row % GRID_W
        first = pl.program_id(1) == 0
        last = pl.program_id(1) == pl.num_programs(1) - 1

        def shifted(xg, d, g0):
            if d == 0:
                return jnp.where(gcol == 0, 0.0, pltpu.roll(xg, 1, 0))
            if d == 1:
                return jnp.where(gcol == GRID_W - 1, 0.0, pltpu.roll(xg, t - 1, 0))
            if d == 2:
                halo = jnp.where(first, 0.0, prev_ref[0, :, g0:g0 + 128].astype(F32))
                return jnp.concatenate([halo, xg[:t - GRID_W]], axis=0)
            halo = jnp.where(last, 0.0, next_ref[0, :, g0:g0 + 128].astype(F32))
            return jnp.concatenate([xg[GRID_W:], halo], axis=0)
    else:
        def shifted(xg, d, g0):
            if d == 0:
                return jnp.where(row == 0, 0.0, pltpu.roll(xg, 1, 0))
            return jnp.where(row == t - 1, 0.0, pltpu.roll(xg, t - 1, 0))

    u = _shift_lerp(x, mu_ref, shifted, segments, n_cols)
    w = width
    r, k, v = u[:, :w], u[:, w:2 * w], u[:, 2 * w:3 * w]
    lo_wa = u[:, 3 * w:3 * w + 128]
    lo_gv = u[:, 3 * w + 128:3 * w + 384]

    lp = LORA_PASSES
    logw = -np.exp(-0.5) * _sigmoid(w0_ref[...] + _mm(jnp.tanh(lo_wa), wup_ref[...], passes=lp))
    iclr = _sigmoid(a0_ref[...] + _mm(lo_wa, aup_ref[...], passes=lp))
    gate = _mm(_sigmoid(lo_gv), gup_ref[...], passes=lp)
    if has_vres:
        mix = _sigmoid(v0_ref[...] + _mm(lo_gv, vup_ref[...], passes=lp))
        v = v + (vfirst_ref[0].astype(F32) - v) * mix

    ones = ones_ref[...]
    kk = k * kk_ref[...]
    kk = kk * lax.rsqrt(jnp.maximum(_mm_exact_rhs(kk * kk, ones, HEAD_SUM_PIECES), 1e-24))
    bonus = _mm_exact_rhs(r * k * rk_ref[...], ones, HEAD_SUM_PIECES) * v

    for o_ref, val in ((r_o, r), (v_o, v), (kk_o, kk), (bonus_o, bonus), (gate_o, gate)):
        o_ref[0] = val.astype(o_ref.dtype)
    ka = ka_ref[...]
    for d, (lw_o, k_o, b_o) in enumerate(((lwf_o, kf_o, bf_o), (lwb_o, kb_o, bb_o))):
        a = iclr[:, d * w:(d + 1) * w]
        lw_o[0] = logw[:, d * w:(d + 1) * w]
        k_o[0] = (k * (1.0 + (a - 1.0) * ka)).astype(k_o.dtype)
        b_o[0] = (kk * a).astype(b_o.dtype)


def _rwkv_prep(p, vfirst, consts, quad, tm):
    b, l, n_cols = p.shape
    w = consts["k_k"].shape[-1]
    has_vres = vfirst is not None
    n_rwkv = consts["n_rwkv"]
    n_vres = consts["n_vres"]
    parts = 4 if quad else 2
    segments = _direction_segments(n_rwkv, parts, 0, w) + _direction_segments(n_vres, parts, n_rwkv, w)
    kern = functools.partial(_rwkv_prep_kernel, quad=quad, has_vres=has_vres, segments=segments,
                             width=w, n_cols=n_cols)
    rows = tm // GRID_W
    n_rows = l // GRID_W
    in_specs = [pl.BlockSpec((1, tm, n_cols), lambda i, j: (i, j, 0))]
    args = [p]
    if quad:
        in_specs += [
            pl.BlockSpec((1, GRID_W, n_cols), lambda i, j: (i, jnp.maximum(j * rows - 1, 0), 0)),
            pl.BlockSpec((1, GRID_W, n_cols), lambda i, j: (i, jnp.minimum((j + 1) * rows, n_rows - 1), 0)),
        ]
        args += [p, p]
    if has_vres:
        in_specs.append(pl.BlockSpec((1, tm, w), lambda i, j: (i, j, 0)))
        args.append(vfirst)
    for name in ("mu", "w0", "w_up", "a0", "a_up", "g_up", "v_up", "v0", "k_k", "k_a", "r_k", "ones"):
        arr = consts[name]
        in_specs.append(pl.BlockSpec(arr.shape, lambda i, j: (0, 0)))
        args.append(arr)
    return pl.pallas_call(
        kern,
        grid=(b, l // tm),
        in_specs=in_specs,
        out_specs=[pl.BlockSpec((1, tm, w), lambda i, j: (i, j, 0))] * 11,
        out_shape=[jax.ShapeDtypeStruct((b, l, w), ACT)] * 5 + [jax.ShapeDtypeStruct((b, l, w), F32)] * 2
        + [jax.ShapeDtypeStruct((b, l, w), ACT)] * 4,
        compiler_params=_params("arbitrary", "arbitrary"),
        name="rwkv_prep",
    )(*args)


def _block_diag(y, block=HEAD_DIM):
    blk = lax.broadcasted_iota(jnp.int32, y.shape, 1) // block
    return jnp.concatenate([jnp.where(blk == h, y, 0.0) for h in range(y.shape[1] // block)], axis=0)


def _head_diag_mask(width):
    r = lax.broadcasted_iota(jnp.int32, (width, width), 0) // HEAD_DIM
    c = lax.broadcasted_iota(jnp.int32, (width, width), 1) // HEAD_DIM
    return r == c


def _rwkv_stream(r, v, kk, lw, kd, bd, reverse, state, out):
    r, v, kk, kd, bd = (t.astype(F32) for t in (r, v, kk, kd, bd))
    c = r.shape[0]
    t_idx = lax.broadcasted_iota(jnp.int32, (c, c), 0)
    s_idx = lax.broadcasted_iota(jnp.int32, (c, c), 1)
    tri = jnp.where((s_idx >= t_idx) if reverse else (s_idx <= t_idx), 1.0, 0.0).astype(BF16)
    lc = _mm_exact_lhs(tri, lw)
    l_end = lc[0:1] if reverse else lc[c - 1:c]
    e_inv = jnp.exp(-lc)
    to_end = jnp.exp(l_end - lc)
    heads = RWKV_GROUP // HEAD_DIM
    row = lax.broadcasted_iota(jnp.int32, (c, heads * c), 0)
    col = lax.broadcasted_iota(jnp.int32, (c, heads * c), 1) % c
    return dict(
        state=state, out=out,
        v=v, r_t=r * jnp.exp(lc), a_t=-kk * jnp.exp(lc - lw), k_t=kd * e_inv, b_t=bd * e_inv,
        k_hat=kd * to_end, b_hat=bd * to_end, p_end=jnp.exp(l_end),
        before=(col > row) if reverse else (col < row),
        upto=(col >= row) if reverse else (col <= row),
        eye=jnp.where(col == row, 1.0, 0.0))


def _rwkv_step(streams, s_scr, passes):
    c = streams[0]["v"].shape[0]
    gw = RWKV_GROUP
    assert c == HEAD_DIM
    chains = [(k, p) for k in range(len(streams)) for p in range(streams[0]["v"].shape[1] // gw)]
    sl = lambda p: slice(p * gw, (p + 1) * gw)
    diag = _head_diag_mask(gw)

    z, zs, s0 = {}, {}, {}
    for ch in chains:
        st, p = streams[ch[0]], ch[1]
        z[ch] = jnp.concatenate([st["a_t"][:, sl(p)], st["r_t"][:, sl(p)]], axis=0)
        s0[ch] = s_scr[st["state"] + (p,)]
        zs[ch] = _mm(z[ch], s0[ch], NT, passes["state"])
    l_ab, m_rb, l_ak, m_rk = {}, {}, {}, {}
    for ch in chains:
        st, p = streams[ch[0]], ch[1]
        w_bk = jnp.concatenate([_block_diag(st["b_t"][:, sl(p)]), _block_diag(st["k_t"][:, sl(p)])], axis=0)
        gram = _mm(z[ch], w_bk, NT, passes["gram"])
        nb = gram.shape[1] // 2
        l_ab[ch] = jnp.where(st["before"], gram[:c, :nb], 0.0)
        m_rb[ch] = jnp.where(st["upto"], gram[c:, :nb], 0.0)
        l_ak[ch] = jnp.where(st["before"], gram[:c, nb:], 0.0)
        m_rk[ch] = jnp.where(st["upto"], gram[c:, nb:], 0.0)

    inv = {ch: streams[ch[0]]["eye"] + l_ab[ch] for ch in chains}
    power = {ch: _mm(l_ab[ch], _block_diag(l_ab[ch], c), NN, passes["inverse"]) for ch in chains}
    kv = {ch: _mm(jnp.concatenate([l_ak[ch], m_rk[ch]], axis=0),
                  _block_diag(streams[ch[0]]["v"][:, sl(ch[1])]), NN, passes["kv"]) for ch in chains}
    for _ in range(int(np.log2(c)) - 2):
        for ch in chains:
            w_ip = jnp.concatenate([_block_diag(inv[ch], c), _block_diag(power[ch], c)], axis=1)
            prod = _mm(power[ch], w_ip, NN, passes["inverse"])
            nb = prod.shape[1] // 2
            inv[ch] = inv[ch] + prod[:, :nb]
            power[ch] = prod[:, nb:]
    for ch in chains:
        inv[ch] = inv[ch] + _mm(power[ch], _block_diag(inv[ch], c), NN, passes["inverse"])

    u = {ch: _mm(inv[ch], _block_diag(zs[ch][:c] + kv[ch][:c]), NN, passes["solve"]) for ch in chains}
    for ch in chains:
        st, p = streams[ch[0]], ch[1]
        y_ref, g = st["out"]
        y_ref[g, :, sl(p)] = zs[ch][c:] + kv[ch][c:] + _mm(m_rb[ch], _block_diag(u[ch]), NN, passes["out"])
    for ch in chains:
        st, p = streams[ch[0]], ch[1]
        uv_t = jnp.concatenate([u[ch], st["v"][:, sl(p)]], axis=0).T
        upd = _mm(uv_t, jnp.concatenate([st["b_hat"][:, sl(p)], st["k_hat"][:, sl(p)]], axis=0), NN,
                  passes["update"])
        s_scr[st["state"] + (p,)] = s0[ch] * st["p_end"][:, sl(p)] + jnp.where(diag, upd, 0.0)


def _rwkv_scan_kernel(rf, vf, kkf, lwf, kf, bf, rb, vb, kkb, lwb, kb, bb, s0_ref,
                      yf_ref, yb_ref, sout_ref, s_scr, *, passes):
    i = pl.program_id(1)

    @pl.when(i == 0)
    def _():
        s_scr[...] = s0_ref[...]

    streams = []
    for g in range(rf.shape[0]):
        streams.append(_rwkv_stream(rf[g], vf[g], kkf[g], lwf[g], kf[g], bf[g], False, (0, g), (yf_ref, g)))
        streams.append(_rwkv_stream(rb[g], vb[g], kkb[g], lwb[g], kb[g], bb[g], True, (1, g), (yb_ref, g)))
    _rwkv_step(streams, s_scr, passes)

    @pl.when(i == pl.num_programs(1) - 1)
    def _():
        sout_ref[...] = s_scr[...]


def _rwkv_scan(r, v, kk, lwf, lwb, kf, kb, bf, bb, s0):
    b, l, w = r.shape
    c = RWKV_CHUNK
    nc = l // c
    g = SCAN_BATCH if b % SCAN_BATCH == 0 else 1
    fwd = pl.BlockSpec((g, c, w), lambda i, j: (i, j, 0))
    bwd = pl.BlockSpec((g, c, w), lambda i, j: (i, nc - 1 - j, 0))
    st = pl.BlockSpec((2, g) + s0.shape[2:], lambda i, j: (0, i, 0, 0, 0))
    return pl.pallas_call(
        functools.partial(_rwkv_scan_kernel, passes=dict(RWKV_PASSES)),
        grid=(b // g, nc),
        in_specs=[fwd] * 6 + [bwd] * 6 + [st],
        out_specs=[fwd, bwd, st],
        out_shape=[jax.ShapeDtypeStruct((b, l, w), F32)] * 2 + [jax.ShapeDtypeStruct(s0.shape, F32)],
        scratch_shapes=[pltpu.VMEM((2, g) + s0.shape[2:], F32)],
        compiler_params=_params("arbitrary", "arbitrary"),
        name="rwkv_scan",
    )(r, v, kk, lwf, kf, bf, r, v, kk, lwb, kb, bb, s0)


def _rope(x, cos, sin_signed):
    w = x.shape[1]
    half = HEAD_DIM // 2
    lane = lax.broadcasted_iota(jnp.int32, x.shape, 1) % HEAD_DIM
    rot = jnp.where(lane < half, pltpu.roll(x, w - half, 1), pltpu.roll(x, half, 1))
    return x * cos + rot * sin_signed


def _ret_step(streams, s_scr):
    n_pairs = streams[0]["v"].shape[1] // PAIR
    chains = [(k, p) for k in range(len(streams)) for p in range(n_pairs)]
    sl = lambda p: slice(p * PAIR, (p + 1) * PAIR)
    diag = _head_diag_mask(PAIR)
    scores, cross, s0 = {}, {}, {}
    for ch in chains:
        st, p = streams[ch[0]], ch[1]
        s0[ch] = s_scr[st["state"] + (p,)]
        scores[ch] = _mm(st["q"][:, sl(p)], _block_diag(st["k"][:, sl(p)]), NT) * st["inner"][p]
        cross[ch] = _mm(st["q"][:, sl(p)], s0[ch]) * st["qdec"][:, sl(p)]
    for ch in chains:
        st, p = streams[ch[0]], ch[1]
        o_ref, g = st["out"]
        o_ref[g, :, sl(p)] = _mm(scores[ch], _block_diag(st["v"][:, sl(p)])) + cross[ch]
    for ch in chains:
        st, p = streams[ch[0]], ch[1]
        upd = _mm((st["k"][:, sl(p)] * st["kdec"][:, sl(p)]).T, st["v"][:, sl(p)])
        s_scr[st["state"] + (p,)] = s0[ch] * st["cdec"][:, sl(p)] + jnp.where(diag, upd, 0.0)


def _ret_scan_kernel(qkvf, qkvb, inf_ref, inb_ref, dec_ref, s0_ref,
                     of_ref, ob_ref, sout_ref, s_scr, *, width):
    i = pl.program_id(1)

    @pl.when(i == 0)
    def _():
        s_scr[...] = s0_ref[...]

    w = width
    streams = []
    for g in range(qkvf.shape[0]):
        for d, (qkv, inner_ref, o_ref) in enumerate(((qkvf, inf_ref, of_ref), (qkvb, inb_ref, ob_ref))):
            x = qkv[g]
            streams.append(dict(
                q=x[:, :w], k=x[:, w:2 * w], v=x[:, 2 * w:], inner=inner_ref, qdec=dec_ref[d, 0],
                kdec=dec_ref[d, 1], cdec=dec_ref[d, 2, 0:1], state=(d, g), out=(o_ref, g)))
    _ret_step(streams, s_scr)

    @pl.when(i == pl.num_programs(1) - 1)
    def _():
        sout_ref[...] = s_scr[...]


def _ret_scan(qkv, tables, s0):
    b, l, w3 = qkv.shape
    w = w3 // 3
    c = RET_CHUNK
    nc = l // c
    n_pairs = w // PAIR
    g = SCAN_BATCH if b % SCAN_BATCH == 0 else 1
    inner_f, inner_b, dec = tables
    st = pl.BlockSpec((2, g, n_pairs, PAIR, PAIR), lambda i, j: (0, i, 0, 0, 0))
    const3 = pl.BlockSpec(inner_f.shape, lambda i, j: (0, 0, 0))
    return pl.pallas_call(
        functools.partial(_ret_scan_kernel, width=w),
        grid=(b // g, nc),
        in_specs=[pl.BlockSpec((g, c, w3), lambda i, j: (i, j, 0)),
                  pl.BlockSpec((g, c, w3), lambda i, j: (i, nc - 1 - j, 0)),
                  const3, const3,
                  pl.BlockSpec(dec.shape, lambda i, j: (0, 0, 0, 0)),
                  st],
        out_specs=[pl.BlockSpec((g, c, w), lambda i, j: (i, j, 0)),
                   pl.BlockSpec((g, c, w), lambda i, j: (i, nc - 1 - j, 0)),
                   st],
        out_shape=[jax.ShapeDtypeStruct((b, l, w), F32)] * 2 + [jax.ShapeDtypeStruct(s0.shape, F32)],
        scratch_shapes=[pltpu.VMEM((2, g, n_pairs, PAIR, PAIR), F32)],
        compiler_params=_params("arbitrary", "arbitrary"),
        name="ret_scan",
    )(qkv, qkv, inner_f, inner_b, dec, s0)


def _ret_tables(n_heads):
    c = RET_CHUNK
    lg = np.log(1.0 - 2.0 ** (-5.0 - np.arange(n_heads, dtype=np.float64)))
    idx = np.arange(c, dtype=np.float64)
    diff = idx[:, None] - idx[None, :]

    def inner(log_gamma, reverse):
        dd = -diff if reverse else diff
        m = np.where(dd >= 0, np.exp(log_gamma[:, None, None] * np.maximum(dd, 0.0)), 0.0)
        return m.reshape(n_heads // 2, 2, c, c).transpose(0, 2, 1, 3).reshape(n_heads // 2, c, 2 * c)

    def per_lane(tab):
        return np.repeat(tab, HEAD_DIM, axis=1)

    dec = []
    for reverse in (False, True):
        g = lg[::-1] if reverse else lg
        pos = (c - 1.0 - idx) if reverse else idx
        qd = per_lane(np.exp(g[None, :] * (pos[:, None] + 1.0)))
        kd = per_lane(np.exp(g[None, :] * (c - 1.0 - pos[:, None])))
        cd = per_lane(np.broadcast_to(np.exp(g * c)[None, :], (c, n_heads)))
        dec.append(np.stack([qd, kd, cd]))
    return (jnp.asarray(inner(lg, False), F32), jnp.asarray(inner(lg[::-1], True), F32),
            jnp.asarray(np.stack(dec), F32))


def _rope_tables(l, n_heads):
    rows = l // GRID_W
    row = jnp.repeat(jnp.arange(rows, dtype=F32), GRID_W)
    col = jnp.tile(jnp.arange(GRID_W, dtype=F32), rows)
    nf = HEAD_DIM // 4
    freqs = ROPE_BASE ** (-jnp.arange(nf, dtype=F32) / nf)
    ang = jnp.concatenate([row[:, None] * freqs, col[:, None] * freqs], axis=-1)
    cos = jnp.tile(jnp.cos(ang), (1, 2 * n_heads))
    sin = jnp.tile(jnp.concatenate([-jnp.sin(ang), jnp.sin(ang)], axis=-1), (1, n_heads))
    return cos, sin


def _head_mean(x):
    outs = []
    for g0 in range(0, x.shape[1], PAIR):
        xg = x[:, g0:g0 + PAIR]
        first = lax.broadcasted_iota(jnp.int32, xg.shape, 1) < HEAD_DIM
        s_first = jnp.sum(jnp.where(first, xg, 0.0), axis=1, keepdims=True)
        s_second = jnp.sum(jnp.where(first, 0.0, xg), axis=1, keepdims=True)
        outs.append(jnp.where(first, s_first, s_second))
    return jnp.concatenate(outs, axis=1) * (1.0 / HEAD_DIM)


def _group_norm(y, w, b, eps):
    d = y - _head_mean(y)
    return d * lax.rsqrt(_head_mean(d * d) + eps) * w + b


def _out_ffn_kernel(x_ref, yf_ref, yb_ref, bonus_ref, gate_ref, of_ref, ob_ref, pg_ref,
                    lnw_ref, lnb_ref, gnw_ref, gnb_ref, wo_ref,
                    g1_ref, sh2_ref, sc2_ref, g2_ref, n2_ref, nf_ref, wfi_ref, wfo_ref,
                    o_ref, mix_scr, *, width, hidden, chunks, final_norm):
    t = pl.program_id(0)
    w = width

    @pl.when(t == 0)
    def _():
        mix_scr[...] = jnp.zeros(mix_scr.shape, mix_scr.dtype)

    slot = lax.rem(t, 2)
    attn = _dot(mix_scr[1 - slot], wo_ref[...])

    o_rwkv = (_group_norm(yf_ref[0] + yb_ref[0], lnw_ref[...], lnb_ref[...], RWKV_GN_EPS)
              + bonus_ref[0]) * gate_ref[0]
    pg = pg_ref[0].astype(F32)
    o_ret = (_group_norm(of_ref[0], gnw_ref[...], gnb_ref[...], RET_GN_EPS) * _silu(pg[:, :w])
             + _group_norm(ob_ref[0], gnw_ref[...], gnb_ref[...], RET_GN_EPS) * _silu(pg[:, w:]))
    mix_scr[slot, :, :w] = o_rwkv.astype(mix_scr.dtype)
    mix_scr[slot, :, w:] = o_ret.astype(mix_scr.dtype)

    x1 = x_ref[0] + g1_ref[0] * attn
    h = _modnorm(x1, n2_ref[...], sh2_ref[0], sc2_ref[0]).astype(BF16)
    acc = None
    for c0, c1 in chunks:
        gate = _dot(h, wfi_ref[:, c0:c1])
        up = _dot(h, wfi_ref[:, hidden + c0:hidden + c1])
        part = _dot((_silu(gate) * up).astype(BF16), wfo_ref[c0:c1])
        acc = part if acc is None else acc + part
    x2 = x1 + g2_ref[0] * acc
    if final_norm:
        ms = jnp.mean(x2 * x2, axis=-1, keepdims=True)
        x2 = (x2 * lax.rsqrt(ms + NORM_EPS)) * nf_ref[...]
    o_ref[0] = x2


def _out_ffn(x, yf, yb, bonus, gate, of, ob, pg, consts, mods, final_norm, tm):
    b, l, d = x.shape
    w = yf.shape[-1]
    hidden = consts["w_ffn_out"].shape[0]
    n_tiles = hidden // MXU_TILE
    n_chunks = -(-n_tiles // FFN_CHUNK_TILES)
    bounds = [MXU_TILE * (n_tiles * i // n_chunks) for i in range(n_chunks + 1)]
    bounds[-1] = hidden
    chunks = tuple(zip(bounds[:-1], bounds[1:]))
    bm = mods[0].shape[0]
    nj = l // tm
    n_tok_tiles = b * nj
    cur = lambda t: jnp.maximum(t - 1, 0)
    nxt = lambda t: jnp.minimum(t, n_tok_tiles - 1)
    mod_map = (lambda t: (cur(t) // nj, 0, 0)) if bm > 1 else (lambda t: (0, 0, 0))
    tok_cur = lambda width: pl.BlockSpec((1, tm, width), lambda t: (cur(t) // nj, cur(t) % nj, 0))
    tok_nxt = lambda width: pl.BlockSpec((1, tm, width), lambda t: (nxt(t) // nj, nxt(t) % nj, 0))
    const = lambda arr: pl.BlockSpec(arr.shape, lambda t: (0,) * arr.ndim, pipeline_mode=pl.Buffered(1))
    names = ("ln_w", "ln_b", "gn_w", "gn_b", "w_out")
    tail = ("norm2", "norm_f", "w_ffn_in", "w_ffn_out")
    in_specs = ([tok_cur(d)] + [tok_nxt(w)] * 6 + [tok_nxt(2 * w)] + [const(consts[n]) for n in names]
                + [pl.BlockSpec((1, 1, d), mod_map)] * 4 + [const(consts[n]) for n in tail])
    args = ([x, yf, yb, bonus, gate, of, ob, pg] + [consts[n] for n in names] + list(mods)
            + [consts[n] for n in tail])
    kern = functools.partial(_out_ffn_kernel, width=w, hidden=hidden, chunks=chunks, final_norm=final_norm)
    return pl.pallas_call(
        kern,
        grid=(n_tok_tiles + 1,),
        in_specs=in_specs,
        out_specs=tok_cur(d),
        out_shape=jax.ShapeDtypeStruct((b, l, d), F32),
        scratch_shapes=[pltpu.VMEM((2, tm, 2 * w), BF16)],
        compiler_params=_params("arbitrary"),
        name="out_ffn",
    )(*args)


def _pad_rows(a, rows, offset=0):
    out = jnp.zeros((rows,) + a.shape[1:], a.dtype)
    return out.at[offset:offset + a.shape[0]].set(a)


def _layer_consts(i, w_in, w_vres_down, mu_rwkv, mu_vres, w0, w_up, a0, a_up, g_up, k_k, k_a, r_k, v0, v_up,
                  ln_x_w, ln_x_b, ret_gn_w, ret_gn_b, w_out, w_ffn_in, w_ffn_out, norm2, norm_f):
    width = w0.shape[-1]
    n_decay, n_iclr, n_gate = w_up.shape[2], a_up.shape[2], g_up.shape[1]
    n_vres = v_up.shape[1]
    n_rwkv = 3 * width + n_decay + n_iclr + n_gate
    assert n_decay + n_iclr == 128 and n_gate + n_vres <= 256
    rwkv_cols = 3 * width + 128 + 256
    d = w_in.shape[1]
    has_vres = i > 0
    wi = w_in[i]
    pad = jnp.zeros((d, rwkv_cols - n_rwkv - n_vres), F32)
    vres_cols = w_vres_down[i - 1] if has_vres else jnp.zeros((d, n_vres), F32)
    w_proj = jnp.concatenate([wi[:, :n_rwkv], vres_cols, pad, wi[:, n_rwkv:]], axis=1).astype(BF16)
    mu_v = mu_vres[i - 1] if has_vres else jnp.zeros((n_vres,), F32)
    mu = jnp.concatenate([mu_rwkv[i], mu_v, jnp.zeros((rwkv_cols - n_rwkv - n_vres,), F32)])[None]
    cat2 = lambda a: jnp.concatenate([a[0], a[1]], axis=-1)
    head = jnp.arange(width) // HEAD_DIM
    ones = (head[:, None] == head[None, :]).astype(BF16)
    row = lambda a: a.reshape(1, -1)
    return dict(
        w_proj=w_proj, widths=(rwkv_cols, 3 * width, 2 * width), n_rwkv=n_rwkv, n_vres=n_vres, mu=mu,
        w0=row(cat2(w0[i])), w_up=_pad_rows(cat2(w_up[i]), 128), a0=row(cat2(a0[i])),
        a_up=_pad_rows(cat2(a_up[i]), 128, n_decay), g_up=_pad_rows(g_up[i], 256),
        v_up=_pad_rows(v_up[i - 1] if has_vres else jnp.zeros((n_vres, width), F32), 256, n_gate),
        v0=row(v0[i - 1] if has_vres else jnp.zeros((width,), F32)),
        k_k=row(k_k[i]), k_a=row(k_a[i]), r_k=row(r_k[i]), ones=ones,
        ln_w=row(ln_x_w[i]), ln_b=row(ln_x_b[i]), gn_w=row(ret_gn_w[i]), gn_b=row(ret_gn_b[i]),
        w_out=w_out[i].astype(BF16), w_ffn_in=w_ffn_in[i].astype(BF16), w_ffn_out=w_ffn_out[i].astype(BF16),
        norm2=row(norm2[i]), norm_f=row(norm_f),
    )


def _mixers(x, norm1, shift, scale, consts, vfirst, s_rwkv, s_ret, rope, tables, quad, tm):
    p_rwkv, p_qkv, p_g = _in_proj(x, norm1, shift, scale, consts["w_proj"], consts["widths"], rope, tm)
    r, v, kk, bonus, gate, lwf, lwb, kf, kb, bf, bb = _rwkv_prep(
        p_rwkv, vfirst, consts, quad, min(TM_PREP, x.shape[1]) if quad else x.shape[1])
    yf, yb, s_rwkv = _rwkv_scan(r, v, kk, lwf, lwb, kf, kb, bf, bb, s_rwkv)
    of, ob, s_ret = _ret_scan(p_qkv, tables, s_ret)
    return (yf, yb, bonus, gate, of, ob, p_g), v, s_rwkv, s_ret


def kernel(x, c, ctx, c_ctx, w_ada, b_ada, norm1, norm2, norm_f, w_in, w_vres_down, mu_rwkv, mu_vres, w0, w_up, a0, a_up, g_up, k_k, k_a, r_k, v0, v_up, ln_x_w, ln_x_b, ret_gn_w, ret_gn_b, w_out, w_ffn_in, w_ffn_out):
    b, l, d = x.shape
    l_ctx = ctx.shape[1]
    depth = w_in.shape[0]
    width = w0.shape[-1]
    n_heads = width // HEAD_DIM
    n_pairs = n_heads // 2
    tm = min(TM_PROJ, l)
    tm_ctx = min(TM_PROJ, l_ctx)
    tm_ffn = min(TM_FFN, l)
    tm_ffn_ctx = min(TM_FFN, l_ctx)

    m_pad = -(-(b + 1) // 8) * 8
    cond = jnp.zeros((m_pad, d), F32).at[:b].set(c).at[b].set(c_ctx)
    mod = _ada(cond, w_ada, b_ada)

    tables = _ret_tables(n_heads)
    rope_l = _rope_tables(l, n_heads)
    rope_c = None
    zero_ret = jnp.zeros((2, b, width // PAIR, PAIR, PAIR), F32)
    zero_rwkv = jnp.zeros((2, b, width // RWKV_GROUP, RWKV_GROUP, RWKV_GROUP), F32)

    vfirst_c = vfirst_l = None
    for i in range(depth):
        last = i == depth - 1
        consts = _layer_consts(i, w_in, w_vres_down, mu_rwkv, mu_vres, w0, w_up, a0, a_up, g_up, k_k, k_a, r_k,
                               v0, v_up, ln_x_w, ln_x_b, ret_gn_w, ret_gn_b, w_out, w_ffn_in, w_ffn_out,
                               norm2, norm_f)
        mods_l = [mod[i, :b, None, j * d:(j + 1) * d] for j in range(6)]
        mods_c = [mod[i, b:b + 1, None, j * d:(j + 1) * d] for j in range(6)]
        n1 = norm1[i][None]

        mix_c, v_c, s_rwkv, s_ret = _mixers(ctx, n1, mods_c[0], mods_c[1], consts, vfirst_c,
                                            zero_rwkv, zero_ret, rope_c, tables, False, tm_ctx)
        mix_l, v_l, _, _ = _mixers(x, n1, mods_l[0], mods_l[1], consts, vfirst_l,
                                   s_rwkv, s_ret, rope_l, tables, True, tm)
        if i == 0:
            vfirst_c, vfirst_l = v_c, v_l
        x = _out_ffn(x, *mix_l, consts, (mods_l[2], mods_l[3], mods_l[4], mods_l[5]), last, tm_ffn)
        if not last:
            ctx = _out_ffn(ctx, *mix_c, consts, (mods_c[2], mods_c[3], mods_c[4], mods_c[5]), False, tm_ffn_ctx)
    return x
```

```python
import functools

import numpy as np
import jax
import jax.numpy as jnp
from jax import lax
from jax.experimental import pallas as pl
from jax.experimental.pallas import tpu as pltpu

F32 = jnp.float32
BF16 = jnp.bfloat16
ACT = BF16

GRID_W = 64
HEAD_DIM = 64
PAIR = 2 * HEAD_DIM
RWKV_GROUP = 2 * HEAD_DIM
RWKV_CHUNK = 64
RET_CHUNK = 128
ROPE_BASE = 10000.0
NORM_EPS = 1e-6
RWKV_GN_EPS = 64e-5
RET_GN_EPS = 1e-5
VMEM_LIMIT = 56 * 1024 * 1024
TM_PROJ = 512
TM_FFN = 512
MXU_TILE = 256
FFN_CHUNK_TILES = 3
SCAN_BATCH = 4
LORA_PASSES = 1
RWKV_PASSES = dict(state=1, gram=1, inverse=1, kv=1, solve=1, out=1, update=1)

NN = ((1,), (0,))
NT = ((1,), (1,))


def _dot(a, b, dims=NN):
    return lax.dot_general(a, b, (dims, ((), ())), preferred_element_type=F32)


def _split(a):
    hi = a.astype(BF16)
    lo = (a - hi.astype(F32)).astype(BF16)
    return hi, lo


def _mm(a, b, dims=NN, passes=1):
    if passes == 1:
        return _dot(a.astype(BF16), b.astype(BF16), dims)
    ah, al = _split(a)
    bh, bl = _split(b)
    lhs = jnp.concatenate([ah, al, ah], axis=1)
    rhs = jnp.concatenate([bh, bh, bl], axis=0 if dims == NN else 1)
    return _dot(lhs, rhs, dims)


def _mm_exact_lhs(a_bf16, b, pieces=3):
    acc = None
    rem = b
    for _ in range(pieces):
        p = rem.astype(BF16)
        t = _dot(a_bf16, p)
        acc = t if acc is None else acc + t
        rem = rem - p.astype(F32)
    return acc


def _sigmoid(x):
    return 1.0 / (1.0 + jnp.exp(-x))


def _silu(x):
    return x * _sigmoid(x)


def _params(*sem):
    return pltpu.CompilerParams(dimension_semantics=sem, vmem_limit_bytes=VMEM_LIMIT)


def _ada_kernel(c_ref, w_ref, b_ref, o_ref):
    o_ref[0] = _mm(_silu(c_ref[...]), w_ref[0], passes=3) + b_ref[0]


def _ada(cond, w_ada, b_ada):
    depth, d, n = w_ada.shape
    m = cond.shape[0]
    tn = n // 4
    return pl.pallas_call(
        _ada_kernel,
        grid=(depth, n // tn),
        in_specs=[pl.BlockSpec((m, d), lambda i, j: (0, 0)),
                  pl.BlockSpec((1, d, tn), lambda i, j: (i, 0, j)),
                  pl.BlockSpec((1, 1, tn), lambda i, j: (i, 0, j))],
        out_specs=pl.BlockSpec((1, m, tn), lambda i, j: (i, 0, j)),
        out_shape=jax.ShapeDtypeStruct((depth, m, n), F32),
        compiler_params=_params("arbitrary", "arbitrary"),
        name="ada",
    )(cond, w_ada, b_ada.reshape(depth, 1, n))


def _modnorm(x, g, shift, scale):
    ms = jnp.mean(x * x, axis=-1, keepdims=True)
    y = (x * lax.rsqrt(ms + NORM_EPS)) * g
    return y * (1.0 + scale) + shift


def _direction_segments(n_cols, parts, offset):
    q = n_cols // parts
    return [(offset + i * q, offset + (i + 1) * q, i) for i in range(parts)]


def _shift_lerp(p_ref, mu_ref, shifted_fn, segments):
    outs = []
    for g0 in range(0, p_ref.shape[1], 128):
        xg = p_ref[:, g0:g0 + 128]
        lane = lax.broadcasted_iota(jnp.int32, xg.shape, 1) + g0
        s = None
        for (c0, c1, d) in segments:
            lo, hi = max(c0, g0), min(c1, g0 + 128)
            if lo >= hi:
                continue
            sd = shifted_fn(xg, d, g0)
            if lo == g0 and hi == g0 + 128:
                s = sd
            else:
                part = jnp.where((lane >= lo) & (lane < hi), sd, 0.0)
                s = part if s is None else s + part
        if s is None:
            s = jnp.zeros_like(xg)
        outs.append(xg + (s - xg) * mu_ref[:, g0:g0 + 128])
    return jnp.concatenate(outs, axis=1)


def _head_sum(x):
    outs = []
    for g0 in range(0, x.shape[1], PAIR):
        xg = x[:, g0:g0 + PAIR]
        first = lax.broadcasted_iota(jnp.int32, xg.shape, 1) < HEAD_DIM
        s_first = jnp.sum(jnp.where(first, xg, 0.0), axis=1, keepdims=True)
        s_second = jnp.sum(jnp.where(first, 0.0, xg), axis=1, keepdims=True)
        outs.append(jnp.where(first, s_first, s_second))
    return jnp.concatenate(outs, axis=1)


def _proj_prep_kernel(*refs, quad, rope, has_vres, segments, width, cols):
    it = iter(refs)
    x_ref = next(it)
    xp_ref = next(it) if quad else None
    xn_ref = next(it) if quad else None
    g_ref, sh_ref, sc_ref, w_ref = (next(it) for _ in range(4))
    cos_ref = next(it) if rope else None
    sin_ref = next(it) if rope else None
    vfirst_ref = next(it) if has_vres else None
    (mu_ref, w0_ref, wup_ref, a0_ref, aup_ref, gup_ref, vup_ref, v0_ref, kk_ref, ka_ref,
     rk_ref) = (next(it) for _ in range(11))
    qkv_o, pg_o = next(it), next(it)
    (r_o, v_o, kk_o, bonus_o, gate_o, lwf_o, lwb_o, kf_o, kb_o, bf_o, bb_o) = (next(it) for _ in range(11))
    p_scr = next(it)
    n_rwkv, n_qkv, n_g = cols
    w = width

    g, sh, sc = g_ref[...], sh_ref[0], sc_ref[0]
    h = _modnorm(x_ref[0], g, sh, sc).astype(BF16)
    for c0 in range(0, n_rwkv, w):
        c1 = min(c0 + w, n_rwkv)
        p_scr[:, c0:c1] = _dot(h, w_ref[:, c0:c1])
    for j in range(3):
        y = _dot(h, w_ref[:, n_rwkv + j * w:n_rwkv + (j + 1) * w])
        if rope and j < 2:
            y = _rope(y, cos_ref[...], sin_ref[...])
        if j == 1:
            y = y * HEAD_DIM ** -0.5
        qkv_o[0, :, j * w:(j + 1) * w] = y.astype(qkv_o.dtype)
    for c0 in range(0, n_g, w):
        pg_o[0, :, c0:c0 + w] = _dot(h, w_ref[:, n_rwkv + n_qkv + c0:n_rwkv + n_qkv + c0 + w]).astype(pg_o.dtype)

    t = x_ref.shape[1]
    row = lax.broadcasted_iota(jnp.int32, (t, 128), 0)
    if quad:
        gcol = row % GRID_W
        first = pl.program_id(1) == 0
        last = pl.program_id(1) == pl.num_programs(1) - 1
        h_above = _modnorm(xp_ref[0], g, sh, sc).astype(BF16)
        h_below = _modnorm(xn_ref[0], g, sh, sc).astype(BF16)

        def shifted(xg, d, g0):
            if d == 0:
                return jnp.where(gcol == 0, 0.0, pltpu.roll(xg, 1, 0))
            if d == 1:
                return jnp.where(gcol == GRID_W - 1, 0.0, pltpu.roll(xg, t - 1, 0))
            if d == 2:
                halo = jnp.where(first, 0.0, _dot(h_above, w_ref[:, g0:g0 + 128]))
                return jnp.concatenate([halo, xg[:t - GRID_W]], axis=0)
            halo = jnp.where(last, 0.0, _dot(h_below, w_ref[:, g0:g0 + 128]))
            return jnp.concatenate([xg[GRID_W:], halo], axis=0)
    else:
        def shifted(xg, d, g0):
            if d == 0:
                return jnp.where(row == 0, 0.0, pltpu.roll(xg, 1, 0))
            return jnp.where(row == t - 1, 0.0, pltpu.roll(xg, t - 1, 0))

    u = _shift_lerp(p_scr, mu_ref, shifted, segments)
    r, k, v = u[:, :w], u[:, w:2 * w], u[:, 2 * w:3 * w]
    lo_wa = u[:, 3 * w:3 * w + 128]
    lo_gv = u[:, 3 * w + 128:3 * w + 384]

    lp = LORA_PASSES
    logw = -np.exp(-0.5) * _sigmoid(w0_ref[...] + _mm(jnp.tanh(lo_wa), wup_ref[...], passes=lp))
    iclr = _sigmoid(a0_ref[...] + _mm(lo_wa, aup_ref[...], passes=lp))
    gate = _mm(_sigmoid(lo_gv), gup_ref[...], passes=lp)
    if has_vres:
        mix = _sigmoid(v0_ref[...] + _mm(lo_gv, vup_ref[...], passes=lp))
        v = v + (vfirst_ref[0].astype(F32) - v) * mix

    kk = k * kk_ref[...]
    kk = kk * lax.rsqrt(jnp.maximum(_head_sum(kk * kk), 1e-24))
    bonus = _head_sum(r * k * rk_ref[...]) * v

    for o_ref, val in ((r_o, r), (v_o, v), (kk_o, kk), (bonus_o, bonus), (gate_o, gate)):
        o_ref[0] = val.astype(o_ref.dtype)
    ka = ka_ref[...]
    for d, (lw_o, k_o, b_o) in enumerate(((lwf_o, kf_o, bf_o), (lwb_o, kb_o, bb_o))):
        a = iclr[:, d * w:(d + 1) * w]
        lw_o[0] = logw[:, d * w:(d + 1) * w]
        k_o[0] = (k * (1.0 + (a - 1.0) * ka)).astype(k_o.dtype)
        b_o[0] = (kk * a).astype(b_o.dtype)


def _proj_prep(x, g, shift, scale, consts, rope, vfirst, quad, tm):
    b, l, d = x.shape
    w = consts["k_k"].shape[-1]
    cols = consts["widths"]
    assert cols[1] == 3 * w and cols[2] % w == 0
    assert quad or tm == l
    has_vres = vfirst is not None
    parts = 4 if quad else 2
    segments = (_direction_segments(consts["n_rwkv"], parts, 0)
                + _direction_segments(consts["n_vres"], parts, consts["n_rwkv"]))
    kern = functools.partial(_proj_prep_kernel, quad=quad, rope=rope is not None, has_vres=has_vres,
                             segments=segments, width=w, cols=cols)
    bm = shift.shape[0]
    mod_map = (lambda i, j: (i, 0, 0)) if bm > 1 else (lambda i, j: (0, 0, 0))
    const = lambda arr: pl.BlockSpec(arr.shape, lambda i, j: (0,) * arr.ndim, pipeline_mode=pl.Buffered(1))
    tok = lambda width: pl.BlockSpec((1, tm, width), lambda i, j: (i, j, 0))
    rows = tm // GRID_W
    n_rows = l // GRID_W
    in_specs = [tok(d)]
    args = [x]
    if quad:
        in_specs += [
            pl.BlockSpec((1, GRID_W, d), lambda i, j: (i, jnp.maximum(j * rows - 1, 0), 0)),
            pl.BlockSpec((1, GRID_W, d), lambda i, j: (i, jnp.minimum((j + 1) * rows, n_rows - 1), 0)),
        ]
        args += [x, x]
    in_specs += [const(g), pl.BlockSpec((1, 1, d), mod_map), pl.BlockSpec((1, 1, d), mod_map),
                 const(consts["w_proj"])]
    args += [g, shift, scale, consts["w_proj"]]
    if rope is not None:
        in_specs += [pl.BlockSpec((tm, w), lambda i, j: (j, 0))] * 2
        args += list(rope)
    if has_vres:
        in_specs.append(tok(w))
        args.append(vfirst)
    for name in ("mu", "w0", "w_up", "a0", "a_up", "g_up", "v_up", "v0", "k_k", "k_a", "r_k"):
        in_specs.append(const(consts[name]))
        args.append(consts[name])
    act = lambda width: jax.ShapeDtypeStruct((b, l, width), ACT)
    f32 = lambda width: jax.ShapeDtypeStruct((b, l, width), F32)
    return pl.pallas_call(
        kern,
        grid=(b, l // tm),
        in_specs=in_specs,
        out_specs=[tok(cols[1]), tok(cols[2])] + [tok(w)] * 11,
        out_shape=[act(cols[1]), act(cols[2])] + [act(w)] * 5 + [f32(w)] * 2 + [act(w)] * 4,
        scratch_shapes=[pltpu.VMEM((tm, cols[0]), F32)],
        compiler_params=_params("arbitrary", "arbitrary"),
        name="proj_prep",
    )(*args)


def _block_diag(y, block=HEAD_DIM):
    blk = lax.broadcasted_iota(jnp.int32, y.shape, 1) // block
    return jnp.concatenate([jnp.where(blk == h, y, 0.0) for h in range(y.shape[1] // block)], axis=0)


def _head_diag_mask(width):
    r = lax.broadcasted_iota(jnp.int32, (width, width), 0) // HEAD_DIM
    c = lax.broadcasted_iota(jnp.int32, (width, width), 1) // HEAD_DIM
    return r == c


def _rwkv_stream(r, v, kk, lw, kd, bd, reverse, state, out):
    r, v, kk, kd, bd = (t.astype(F32) for t in (r, v, kk, kd, bd))
    c = r.shape[0]
    t_idx = lax.broadcasted_iota(jnp.int32, (c, c), 0)
    s_idx = lax.broadcasted_iota(jnp.int32, (c, c), 1)
    tri = jnp.where((s_idx >= t_idx) if reverse else (s_idx <= t_idx), 1.0, 0.0).astype(BF16)
    lc = _mm_exact_lhs(tri, lw)
    l_end = lc[0:1] if reverse else lc[c - 1:c]
    e_inv = jnp.exp(-lc)
    to_end = jnp.exp(l_end - lc)
    heads = RWKV_GROUP // HEAD_DIM
    row = lax.broadcasted_iota(jnp.int32, (c, heads * c), 0)
    col = lax.broadcasted_iota(jnp.int32, (c, heads * c), 1) % c
    return dict(
        state=state, out=out,
        v=v, r_t=r * jnp.exp(lc), a_t=-kk * jnp.exp(lc - lw), k_t=kd * e_inv, b_t=bd * e_inv,
        k_hat=kd * to_end, b_hat=bd * to_end, p_end=jnp.exp(l_end),
        before=(col > row) if reverse else (col < row),
        upto=(col >= row) if reverse else (col <= row),
        eye=jnp.where(col == row, 1.0, 0.0))


def _rwkv_step(streams, s_scr, passes):
    c = streams[0]["v"].shape[0]
    gw = RWKV_GROUP
    assert c == HEAD_DIM
    chains = [(k, p) for k in range(len(streams)) for p in range(streams[0]["v"].shape[1] // gw)]
    sl = lambda p: slice(p * gw, (p + 1) * gw)
    diag = _head_diag_mask(gw)

    z, zs, s0 = {}, {}, {}
    for ch in chains:
        st, p = streams[ch[0]], ch[1]
        z[ch] = jnp.concatenate([st["a_t"][:, sl(p)], st["r_t"][:, sl(p)]], axis=0)
        s0[ch] = s_scr[st["state"] + (p,)]
        zs[ch] = _mm(z[ch], s0[ch], NT, passes["state"])
    l_ab, m_rb, l_ak, m_rk = {}, {}, {}, {}
    for ch in chains:
        st, p = streams[ch[0]], ch[1]
        w_bk = jnp.concatenate([_block_diag(st["b_t"][:, sl(p)]), _block_diag(st["k_t"][:, sl(p)])], axis=0)
        gram = _mm(z[ch], w_bk, NT, passes["gram"])
        nb = gram.shape[1] // 2
        l_ab[ch] = jnp.where(st["before"], gram[:c, :nb], 0.0)
        m_rb[ch] = jnp.where(st["upto"], gram[c:, :nb], 0.0)
        l_ak[ch] = jnp.where(st["before"], gram[:c, nb:], 0.0)
        m_rk[ch] = jnp.where(st["upto"], gram[c:, nb:], 0.0)

    inv = {ch: streams[ch[0]]["eye"] + l_ab[ch] for ch in chains}
    power = {ch: _mm(l_ab[ch], _block_diag(l_ab[ch], c), NN, passes["inverse"]) for ch in chains}
    kv = {ch: _mm(jnp.concatenate([l_ak[ch], m_rk[ch]], axis=0),
                  _block_diag(streams[ch[0]]["v"][:, sl(ch[1])]), NN, passes["kv"]) for ch in chains}
    for _ in range(int(np.log2(c)) - 2):
        for ch in chains:
            w_ip = jnp.concatenate([_block_diag(inv[ch], c), _block_diag(power[ch], c)], axis=1)
            prod = _mm(power[ch], w_ip, NN, passes["inverse"])
            nb = prod.shape[1] // 2
            inv[ch] = inv[ch] + prod[:, :nb]
            power[ch] = prod[:, nb:]
    for ch in chains:
        inv[ch] = inv[ch] + _mm(power[ch], _block_diag(inv[ch], c), NN, passes["inverse"])

    u = {ch: _mm(inv[ch], _block_diag(zs[ch][:c] + kv[ch][:c]), NN, passes["solve"]) for ch in chains}
    for ch in chains:
        st, p = streams[ch[0]], ch[1]
        y_ref, g = st["out"]
        y_ref[g, :, sl(p)] = zs[ch][c:] + kv[ch][c:] + _mm(m_rb[ch], _block_diag(u[ch]), NN, passes["out"])
    for ch in chains:
        st, p = streams[ch[0]], ch[1]
        uv_t = jnp.concatenate([u[ch], st["v"][:, sl(p)]], axis=0).T
        upd = _mm(uv_t, jnp.concatenate([st["b_hat"][:, sl(p)], st["k_hat"][:, sl(p)]], axis=0), NN,
                  passes["update"])
        s_scr[st["state"] + (p,)] = s0[ch] * st["p_end"][:, sl(p)] + jnp.where(diag, upd, 0.0)


def _rwkv_scan_kernel(rf, vf, kkf, lwf, kf, bf, rb, vb, kkb, lwb, kb, bb, s0_ref,
                      yf_ref, yb_ref, sout_ref, s_scr, *, passes):
    i = pl.program_id(1)

    @pl.when(i == 0)
    def _():
        s_scr[...] = s0_ref[...]

    streams = []
    for g in range(rf.shape[0]):
        streams.append(_rwkv_stream(rf[g], vf[g], kkf[g], lwf[g], kf[g], bf[g], False, (0, g), (yf_ref, g)))
        streams.append(_rwkv_stream(rb[g], vb[g], kkb[g], lwb[g], kb[g], bb[g], True, (1, g), (yb_ref, g)))
    _rwkv_step(streams, s_scr, passes)

    @pl.when(i == pl.num_programs(1) - 1)
    def _():
        sout_ref[...] = s_scr[...]


def _rwkv_scan(r, v, kk, lwf, lwb, kf, kb, bf, bb, s0):
    b, l, w = r.shape
    c = RWKV_CHUNK
    nc = l // c
    g = SCAN_BATCH if b % SCAN_BATCH == 0 else 1
    fwd = pl.BlockSpec((g, c, w), lambda i, j: (i, j, 0))
    bwd = pl.BlockSpec((g, c, w), lambda i, j: (i, nc - 1 - j, 0))
    st = pl.BlockSpec((2, g) + s0.shape[2:], lambda i, j: (0, i, 0, 0, 0))
    return pl.pallas_call(
        functools.partial(_rwkv_scan_kernel, passes=dict(RWKV_PASSES)),
        grid=(b // g, nc),
        in_specs=[fwd] * 6 + [bwd] * 6 + [st],
        out_specs=[fwd, bwd, st],
        out_shape=[jax.ShapeDtypeStruct((b, l, w), F32)] * 2 + [jax.ShapeDtypeStruct(s0.shape, F32)],
        scratch_shapes=[pltpu.VMEM((2, g) + s0.shape[2:], F32)],
        compiler_params=_params("arbitrary", "arbitrary"),
        name="rwkv_scan",
    )(r, v, kk, lwf, kf, bf, r, v, kk, lwb, kb, bb, s0)


def _rope(x, cos, sin_signed):
    w = x.shape[1]
    half = HEAD_DIM // 2
    lane = lax.broadcasted_iota(jnp.int32, x.shape, 1) % HEAD_DIM
    rot = jnp.where(lane < half, pltpu.roll(x, w - half, 1), pltpu.roll(x, half, 1))
    return x * cos + rot * sin_signed


def _ret_step(streams, s_scr):
    n_pairs = streams[0]["v"].shape[1] // PAIR
    chains = [(k, p) for k in range(len(streams)) for p in range(n_pairs)]
    sl = lambda p: slice(p * PAIR, (p + 1) * PAIR)
    diag = _head_diag_mask(PAIR)
    scores, cross, s0 = {}, {}, {}
    for ch in chains:
        st, p = streams[ch[0]], ch[1]
        s0[ch] = s_scr[st["state"] + (p,)]
        scores[ch] = _mm(st["q"][:, sl(p)], _block_diag(st["k"][:, sl(p)]), NT) * st["inner"][p]
        cross[ch] = _mm(st["q"][:, sl(p)], s0[ch]) * st["qdec"][:, sl(p)]
    for ch in chains:
        st, p = streams[ch[0]], ch[1]
        o_ref, g = st["out"]
        o_ref[g, :, sl(p)] = _mm(scores[ch], _block_diag(st["v"][:, sl(p)])) + cross[ch]
    for ch in chains:
        st, p = streams[ch[0]], ch[1]
        upd = _mm((st["k"][:, sl(p)] * st["kdec"][:, sl(p)]).T, st["v"][:, sl(p)])
        s_scr[st["state"] + (p,)] = s0[ch] * st["cdec"][:, sl(p)] + jnp.where(diag, upd, 0.0)


def _ret_scan_kernel(qkvf, qkvb, inf_ref, inb_ref, dec_ref, s0_ref,
                     of_ref, ob_ref, sout_ref, s_scr, *, width):
    i = pl.program_id(1)

    @pl.when(i == 0)
    def _():
        s_scr[...] = s0_ref[...]

    w = width
    streams = []
    for g in range(qkvf.shape[0]):
        for d, (qkv, inner_ref, o_ref) in enumerate(((qkvf, inf_ref, of_ref), (qkvb, inb_ref, ob_ref))):
            x = qkv[g]
            streams.append(dict(
                q=x[:, :w], k=x[:, w:2 * w], v=x[:, 2 * w:], inner=inner_ref, qdec=dec_ref[d, 0],
                kdec=dec_ref[d, 1], cdec=dec_ref[d, 2, 0:1], state=(d, g), out=(o_ref, g)))
    _ret_step(streams, s_scr)

    @pl.when(i == pl.num_programs(1) - 1)
    def _():
        sout_ref[...] = s_scr[...]


def _ret_scan(qkv, tables, s0):
    b, l, w3 = qkv.shape
    w = w3 // 3
    c = RET_CHUNK
    nc = l // c
    n_pairs = w // PAIR
    g = SCAN_BATCH if b % SCAN_BATCH == 0 else 1
    inner_f, inner_b, dec = tables
    st = pl.BlockSpec((2, g, n_pairs, PAIR, PAIR), lambda i, j: (0, i, 0, 0, 0))
    const3 = pl.BlockSpec(inner_f.shape, lambda i, j: (0, 0, 0))
    return pl.pallas_call(
        functools.partial(_ret_scan_kernel, width=w),
        grid=(b // g, nc),
        in_specs=[pl.BlockSpec((g, c, w3), lambda i, j: (i, j, 0)),
                  pl.BlockSpec((g, c, w3), lambda i, j: (i, nc - 1 - j, 0)),
                  const3, const3,
                  pl.BlockSpec(dec.shape, lambda i, j: (0, 0, 0, 0)),
                  st],
        out_specs=[pl.BlockSpec((g, c, w), lambda i, j: (i, j, 0)),
                   pl.BlockSpec((g, c, w), lambda i, j: (i, nc - 1 - j, 0)),
                   st],
        out_shape=[jax.ShapeDtypeStruct((b, l, w), F32)] * 2 + [jax.ShapeDtypeStruct(s0.shape, F32)],
        scratch_shapes=[pltpu.VMEM((2, g, n_pairs, PAIR, PAIR), F32)],
        compiler_params=_params("arbitrary", "arbitrary"),
        name="ret_scan",
    )(qkv, qkv, inner_f, inner_b, dec, s0)


def _ret_tables(n_heads):
    c = RET_CHUNK
    lg = np.log(1.0 - 2.0 ** (-5.0 - np.arange(n_heads, dtype=np.float64)))
    idx = np.arange(c, dtype=np.float64)
    diff = idx[:, None] - idx[None, :]

    def inner(log_gamma, reverse):
        dd = -diff if reverse else diff
        m = np.where(dd >= 0, np.exp(log_gamma[:, None, None] * np.maximum(dd, 0.0)), 0.0)
        return m.reshape(n_heads // 2, 2, c, c).transpose(0, 2, 1, 3).reshape(n_heads // 2, c, 2 * c)

    def per_lane(tab):
        return np.repeat(tab, HEAD_DIM, axis=1)

    dec = []
    for reverse in (False, True):
        g = lg[::-1] if reverse else lg
        pos = (c - 1.0 - idx) if reverse else idx
        qd = per_lane(np.exp(g[None, :] * (pos[:, None] + 1.0)))
        kd = per_lane(np.exp(g[None, :] * (c - 1.0 - pos[:, None])))
        cd = per_lane(np.broadcast_to(np.exp(g * c)[None, :], (c, n_heads)))
        dec.append(np.stack([qd, kd, cd]))
    return (jnp.asarray(inner(lg, False), F32), jnp.asarray(inner(lg[::-1], True), F32),
            jnp.asarray(np.stack(dec), F32))


def _rope_tables(l, n_heads):
    rows = l // GRID_W
    row = jnp.repeat(jnp.arange(rows, dtype=F32), GRID_W)
    col = jnp.tile(jnp.arange(GRID_W, dtype=F32), rows)
    nf = HEAD_DIM // 4
    freqs = ROPE_BASE ** (-jnp.arange(nf, dtype=F32) / nf)
    ang = jnp.concatenate([row[:, None] * freqs, col[:, None] * freqs], axis=-1)
    cos = jnp.tile(jnp.cos(ang), (1, 2 * n_heads))
    sin = jnp.tile(jnp.concatenate([-jnp.sin(ang), jnp.sin(ang)], axis=-1), (1, n_heads))
    return cos, sin


def _group_norm(y, w, b, eps):
    inv_n = 1.0 / HEAD_DIM
    d = y - _head_sum(y) * inv_n
    return d * lax.rsqrt(_head_sum(d * d) * inv_n + eps) * w + b


def _out_ffn_kernel(x_ref, yf_ref, yb_ref, bonus_ref, gate_ref, of_ref, ob_ref, pg_ref,
                    lnw_ref, lnb_ref, gnw_ref, gnb_ref, wo_ref,
                    g1_ref, sh2_ref, sc2_ref, g2_ref, n2_ref, nf_ref, wfi_ref, wfo_ref,
                    o_ref, mix_scr, *, width, hidden, chunks, final_norm):
    t = pl.program_id(0)
    w = width

    @pl.when(t == 0)
    def _():
        mix_scr[...] = jnp.zeros(mix_scr.shape, mix_scr.dtype)

    slot = lax.rem(t, 2)
    attn = _dot(mix_scr[1 - slot], wo_ref[...])

    o_rwkv = (_group_norm(yf_ref[0] + yb_ref[0], lnw_ref[...], lnb_ref[...], RWKV_GN_EPS)
              + bonus_ref[0]) * gate_ref[0]
    pg = pg_ref[0].astype(F32)
    o_ret = (_group_norm(of_ref[0], gnw_ref[...], gnb_ref[...], RET_GN_EPS) * _silu(pg[:, :w])
             + _group_norm(ob_ref[0], gnw_ref[...], gnb_ref[...], RET_GN_EPS) * _silu(pg[:, w:]))
    mix_scr[slot, :, :w] = o_rwkv.astype(mix_scr.dtype)
    mix_scr[slot, :, w:] = o_ret.astype(mix_scr.dtype)

    x1 = x_ref[0] + g1_ref[0] * attn
    h = _modnorm(x1, n2_ref[...], sh2_ref[0], sc2_ref[0]).astype(BF16)
    acc = None
    for c0, c1 in chunks:
        gate = _dot(h, wfi_ref[:, c0:c1])
        up = _dot(h, wfi_ref[:, hidden + c0:hidden + c1])
        part = _dot((_silu(gate) * up).astype(BF16), wfo_ref[c0:c1])
        acc = part if acc is None else acc + part
    x2 = x1 + g2_ref[0] * acc
    if final_norm:
        ms = jnp.mean(x2 * x2, axis=-1, keepdims=True)
        x2 = (x2 * lax.rsqrt(ms + NORM_EPS)) * nf_ref[...]
    o_ref[0] = x2


def _out_ffn(x, yf, yb, bonus, gate, of, ob, pg, consts, mods, final_norm, tm):
    b, l, d = x.shape
    w = yf.shape[-1]
    hidden = consts["w_ffn_out"].shape[0]
    n_tiles = hidden // MXU_TILE
    n_chunks = -(-n_tiles // FFN_CHUNK_TILES)
    bounds = [MXU_TILE * (n_tiles * i // n_chunks) for i in range(n_chunks + 1)]
    bounds[-1] = hidden
    chunks = tuple(zip(bounds[:-1], bounds[1:]))
    bm = mods[0].shape[0]
    nj = l // tm
    n_tok_tiles = b * nj
    cur = lambda t: jnp.maximum(t - 1, 0)
    nxt = lambda t: jnp.minimum(t, n_tok_tiles - 1)
    mod_map = (lambda t: (cur(t) // nj, 0, 0)) if bm > 1 else (lambda t: (0, 0, 0))
    tok_cur = lambda width: pl.BlockSpec((1, tm, width), lambda t: (cur(t) // nj, cur(t) % nj, 0))
    tok_nxt = lambda width: pl.BlockSpec((1, tm, width), lambda t: (nxt(t) // nj, nxt(t) % nj, 0))
    const = lambda arr: pl.BlockSpec(arr.shape, lambda t: (0,) * arr.ndim, pipeline_mode=pl.Buffered(1))
    names = ("ln_w", "ln_b", "gn_w", "gn_b", "w_out")
    tail = ("norm2", "norm_f", "w_ffn_in", "w_ffn_out")
    in_specs = ([tok_cur(d)] + [tok_nxt(w)] * 6 + [tok_nxt(2 * w)] + [const(consts[n]) for n in names]
                + [pl.BlockSpec((1, 1, d), mod_map)] * 4 + [const(consts[n]) for n in tail])
    args = ([x, yf, yb, bonus, gate, of, ob, pg] + [consts[n] for n in names] + list(mods)
            + [consts[n] for n in tail])
    kern = functools.partial(_out_ffn_kernel, width=w, hidden=hidden, chunks=chunks, final_norm=final_norm)
    return pl.pallas_call(
        kern,
        grid=(n_tok_tiles + 1,),
        in_specs=in_specs,
        out_specs=tok_cur(d),
        out_shape=jax.ShapeDtypeStruct((b, l, d), F32),
        scratch_shapes=[pltpu.VMEM((2, tm, 2 * w), BF16)],
        compiler_params=_params("arbitrary"),
        name="out_ffn",
    )(*args)


def _pad_rows(a, rows, offset=0):
    out = jnp.zeros((rows,) + a.shape[1:], a.dtype)
    return out.at[offset:offset + a.shape[0]].set(a)


def _layer_consts(i, w_in, w_vres_down, mu_rwkv, mu_vres, w0, w_up, a0, a_up, g_up, k_k, k_a, r_k, v0, v_up,
                  ln_x_w, ln_x_b, ret_gn_w, ret_gn_b, w_out, w_ffn_in, w_ffn_out, norm2, norm_f):
    width = w0.shape[-1]
    n_decay, n_iclr, n_gate = w_up.shape[2], a_up.shape[2], g_up.shape[1]
    n_vres = v_up.shape[1]
    n_rwkv = 3 * width + n_decay + n_iclr + n_gate
    assert n_decay + n_iclr == 128 and n_gate + n_vres <= 256
    rwkv_cols = 3 * width + 128 + 256
    d = w_in.shape[1]
    has_vres = i > 0
    wi = w_in[i]
    pad = jnp.zeros((d, rwkv_cols - n_rwkv - n_vres), F32)
    vres_cols = w_vres_down[i - 1] if has_vres else jnp.zeros((d, n_vres), F32)
    w_proj = jnp.concatenate([wi[:, :n_rwkv], vres_cols, pad, wi[:, n_rwkv:]], axis=1).astype(BF16)
    mu_v = mu_vres[i - 1] if has_vres else jnp.zeros((n_vres,), F32)
    mu = jnp.concatenate([mu_rwkv[i], mu_v, jnp.zeros((rwkv_cols - n_rwkv - n_vres,), F32)])[None]
    cat2 = lambda a: jnp.concatenate([a[0], a[1]], axis=-1)
    row = lambda a: a.reshape(1, -1)
    return dict(
        w_proj=w_proj, widths=(rwkv_cols, 3 * width, 2 * width), n_rwkv=n_rwkv, n_vres=n_vres, mu=mu,
        w0=row(cat2(w0[i])), w_up=_pad_rows(cat2(w_up[i]), 128), a0=row(cat2(a0[i])),
        a_up=_pad_rows(cat2(a_up[i]), 128, n_decay), g_up=_pad_rows(g_up[i], 256),
        v_up=_pad_rows(v_up[i - 1] if has_vres else jnp.zeros((n_vres, width), F32), 256, n_gate),
        v0=row(v0[i - 1] if has_vres else jnp.zeros((width,), F32)),
        k_k=row(k_k[i]), k_a=row(k_a[i]), r_k=row(r_k[i]),
        ln_w=row(ln_x_w[i]), ln_b=row(ln_x_b[i]), gn_w=row(ret_gn_w[i]), gn_b=row(ret_gn_b[i]),
        w_out=w_out[i].astype(BF16), w_ffn_in=w_ffn_in[i].astype(BF16), w_ffn_out=w_ffn_out[i].astype(BF16),
        norm2=row(norm2[i]), norm_f=row(norm_f),
    )


def _mixers(x, norm1, shift, scale, consts, vfirst, s_rwkv, s_ret, rope, tables, quad, tm):
    p_qkv, p_g, r, v, kk, bonus, gate, lwf, lwb, kf, kb, bf, bb = _proj_prep(
        x, norm1, shift, scale, consts, rope, vfirst, quad, tm)
    yf, yb, s_rwkv = _rwkv_scan(r, v, kk, lwf, lwb, kf, kb, bf, bb, s_rwkv)
    of, ob, s_ret = _ret_scan(p_qkv, tables, s_ret)
    return (yf, yb, bonus, gate, of, ob, p_g), v, s_rwkv, s_ret


def kernel(x, c, ctx, c_ctx, w_ada, b_ada, norm1, norm2, norm_f, w_in, w_vres_down, mu_rwkv, mu_vres, w0, w_up, a0, a_up, g_up, k_k, k_a, r_k, v0, v_up, ln_x_w, ln_x_b, ret_gn_w, ret_gn_b, w_out, w_ffn_in, w_ffn_out):
    b, l, d = x.shape
    l_ctx = ctx.shape[1]
    depth = w_in.shape[0]
    width = w0.shape[-1]
    n_heads = width // HEAD_DIM
    n_pairs = n_heads // 2
    tm = min(TM_PROJ, l)
    tm_ctx = l_ctx
    tm_ffn = min(TM_FFN, l)
    tm_ffn_ctx = min(TM_FFN, l_ctx)

    m_pad = -(-(b + 1) // 8) * 8
    cond = jnp.zeros((m_pad, d), F32).at[:b].set(c).at[b].set(c_ctx)
    mod = _ada(cond, w_ada, b_ada)

    tables = _ret_tables(n_heads)
    rope_l = _rope_tables(l, n_heads)
    rope_c = None
    zero_ret = jnp.zeros((2, b, width // PAIR, PAIR, PAIR), F32)
    zero_rwkv = jnp.zeros((2, b, width // RWKV_GROUP, RWKV_GROUP, RWKV_GROUP), F32)

    vfirst_c = vfirst_l = None
    for i in range(depth):
        last = i == depth - 1
        consts = _layer_consts(i, w_in, w_vres_down, mu_rwkv, mu_vres, w0, w_up, a0, a_up, g_up, k_k, k_a, r_k,
                               v0, v_up, ln_x_w, ln_x_b, ret_gn_w, ret_gn_b, w_out, w_ffn_in, w_ffn_out,
                               norm2, norm_f)
        mods_l = [mod[i, :b, None, j * d:(j + 1) * d] for j in range(6)]
        mods_c = [mod[i, b:b + 1, None, j * d:(j + 1) * d] for j in range(6)]
        n1 = norm1[i][None]

        mix_c, v_c, s_rwkv, s_ret = _mixers(ctx, n1, mods_c[0], mods_c[1], consts, vfirst_c,
                                            zero_rwkv, zero_ret, rope_c, tables, False, tm_ctx)
        mix_l, v_l, _, _ = _mixers(x, n1, mods_l[0], mods_l[1], consts, vfirst_l,
                                   s_rwkv, s_ret, rope_l, tables, True, tm)
        if i == 0:
            vfirst_c, vfirst_l = v_c, v_l
        x = _out_ffn(x, *mix_l, consts, (mods_l[2], mods_l[3], mods_l[4], mods_l[5]), last, tm_ffn)
        if not last:
            ctx = _out_ffn(ctx, *mix_c, consts, (mods_c[2], mods_c[3], mods_c[4], mods_c[5]), False, tm_ffn_ctx)
    return x
```

```python
import functools

import numpy as np
import jax
import jax.numpy as jnp
from jax import lax
from jax.experimental import pallas as pl
from jax.experimental.pallas import tpu as pltpu

F32 = jnp.float32
BF16 = jnp.bfloat16
ACT = BF16

GRID_W = 64
HEAD_DIM = 64
PAIR = 2 * HEAD_DIM
RWKV_GROUP = 2 * HEAD_DIM
RWKV_CHUNK = 64
RET_CHUNK = 128
ROPE_BASE = 10000.0
NORM_EPS = 1e-6
RWKV_GN_EPS = 64e-5
RET_GN_EPS = 1e-5
VMEM_LIMIT = 60 * 1024 * 1024
TM_PROJ = 512
TM_FFN = 512
MXU_TILE = 256
FFN_CHUNK_TILES = 3
SCAN_BATCH = 4
RET_STAGE_EVERY = 7
LORA_PASSES = 1
RWKV_PASSES = dict(state=1, gram=1, inverse=1, kv=1, solve=1, out=1, update=1)

NN = ((1,), (0,))
NT = ((1,), (1,))


def _dot(a, b, dims=NN):
    return lax.dot_general(a, b, (dims, ((), ())), preferred_element_type=F32)


def _split(a):
    hi = a.astype(BF16)
    lo = (a - hi.astype(F32)).astype(BF16)
    return hi, lo


def _mm(a, b, dims=NN, passes=1):
    if passes == 1:
        return _dot(a.astype(BF16), b.astype(BF16), dims)
    ah, al = _split(a)
    bh, bl = _split(b)
    lhs = jnp.concatenate([ah, al, ah], axis=1)
    rhs = jnp.concatenate([bh, bh, bl], axis=0 if dims == NN else 1)
    return _dot(lhs, rhs, dims)


def _mm_exact_lhs(a_bf16, b, pieces=3):
    acc = None
    rem = b
    for _ in range(pieces):
        p = rem.astype(BF16)
        t = _dot(a_bf16, p)
        acc = t if acc is None else acc + t
        rem = rem - p.astype(F32)
    return acc


def _sigmoid(x):
    return 1.0 / (1.0 + jnp.exp(-x))


def _silu(x):
    return x * _sigmoid(x)


def _params(*sem):
    return pltpu.CompilerParams(dimension_semantics=sem, vmem_limit_bytes=VMEM_LIMIT)


def _ada_kernel(c_ref, w_ref, b_ref, o_ref):
    o_ref[0] = _mm(_silu(c_ref[...]), w_ref[0], passes=3) + b_ref[0]


def _ada(cond, w_ada, b_ada):
    depth, d, n = w_ada.shape
    m = cond.shape[0]
    tn = n // 4
    return pl.pallas_call(
        _ada_kernel,
        grid=(depth, n // tn),
        in_specs=[pl.BlockSpec((m, d), lambda i, j: (0, 0)),
                  pl.BlockSpec((1, d, tn), lambda i, j: (i, 0, j)),
                  pl.BlockSpec((1, 1, tn), lambda i, j: (i, 0, j))],
        out_specs=pl.BlockSpec((1, m, tn), lambda i, j: (i, 0, j)),
        out_shape=jax.ShapeDtypeStruct((depth, m, n), F32),
        compiler_params=_params("arbitrary", "arbitrary"),
        name="ada",
    )(cond, w_ada, b_ada.reshape(depth, 1, n))


def _modnorm(x, g, shift, scale):
    ms = jnp.mean(x * x, axis=-1, keepdims=True)
    y = (x * lax.rsqrt(ms + NORM_EPS)) * g
    return y * (1.0 + scale) + shift


def _direction_segments(n_cols, parts, offset):
    q = n_cols // parts
    return [(offset + i * q, offset + (i + 1) * q, i) for i in range(parts)]


def _shift_lerp(p_ref, mu_ref, shifted_fn, segments):
    outs = []
    for g0 in range(0, p_ref.shape[1], 128):
        xg = p_ref[:, g0:g0 + 128]
        lane = lax.broadcasted_iota(jnp.int32, xg.shape, 1) + g0
        s = None
        for (c0, c1, d) in segments:
            lo, hi = max(c0, g0), min(c1, g0 + 128)
            if lo >= hi:
                continue
            sd = shifted_fn(xg, d, g0)
            if lo == g0 and hi == g0 + 128:
                s = sd
            else:
                part = jnp.where((lane >= lo) & (lane < hi), sd, 0.0)
                s = part if s is None else s + part
        if s is None:
            s = jnp.zeros_like(xg)
        outs.append(xg + (s - xg) * mu_ref[:, g0:g0 + 128])
    return jnp.concatenate(outs, axis=1)


def _head_sum(x):
    outs = []
    for g0 in range(0, x.shape[1], PAIR):
        xg = x[:, g0:g0 + PAIR]
        first = lax.broadcasted_iota(jnp.int32, xg.shape, 1) < HEAD_DIM
        s_first = jnp.sum(jnp.where(first, xg, 0.0), axis=1, keepdims=True)
        s_second = jnp.sum(jnp.where(first, 0.0, xg), axis=1, keepdims=True)
        outs.append(jnp.where(first, s_first, s_second))
    return jnp.concatenate(outs, axis=1)


def _proj_prep_kernel(*refs, quad, rope, has_vres, segments, width, cols):
    it = iter(refs)
    x_ref = next(it)
    xp_ref = next(it) if quad else None
    xn_ref = next(it) if quad else None
    g_ref, sh_ref, sc_ref, w_ref = (next(it) for _ in range(4))
    cos_ref = next(it) if rope else None
    sin_ref = next(it) if rope else None
    vfirst_ref = next(it) if has_vres else None
    (mu_ref, w0_ref, wup_ref, a0_ref, aup_ref, gup_ref, vup_ref, v0_ref, kk_ref, ka_ref,
     rk_ref) = (next(it) for _ in range(11))
    qkv_o, pg_o = next(it), next(it)
    (r_o, v_o, kk_o, bonus_o, gate_o, lwf_o, lwb_o, kf_o, kb_o, bf_o, bb_o) = (next(it) for _ in range(11))
    p_scr = next(it)
    n_rwkv, n_qkv, n_g = cols
    w = width

    g, sh, sc = g_ref[...], sh_ref[0], sc_ref[0]
    h = _modnorm(x_ref[0], g, sh, sc).astype(BF16)
    for c0 in range(0, n_rwkv, w):
        c1 = min(c0 + w, n_rwkv)
        p_scr[:, c0:c1] = _dot(h, w_ref[:, c0:c1])
    for j in range(3):
        y = _dot(h, w_ref[:, n_rwkv + j * w:n_rwkv + (j + 1) * w])
        if rope and j < 2:
            y = _rope(y, cos_ref[...], sin_ref[...])
        if j == 1:
            y = y * HEAD_DIM ** -0.5
        qkv_o[0, :, j * w:(j + 1) * w] = y.astype(qkv_o.dtype)
    for c0 in range(0, n_g, w):
        pg_o[0, :, c0:c0 + w] = _dot(h, w_ref[:, n_rwkv + n_qkv + c0:n_rwkv + n_qkv + c0 + w]).astype(pg_o.dtype)

    t = x_ref.shape[1]
    row = lax.broadcasted_iota(jnp.int32, (t, 128), 0)
    if quad:
        gcol = row % GRID_W
        first = pl.program_id(1) == 0
        last = pl.program_id(1) == pl.num_programs(1) - 1
        h_above = _modnorm(xp_ref[0], g, sh, sc).astype(BF16)
        h_below = _modnorm(xn_ref[0], g, sh, sc).astype(BF16)

        def shifted(xg, d, g0):
            if d == 0:
                return jnp.where(gcol == 0, 0.0, pltpu.roll(xg, 1, 0))
            if d == 1:
                return jnp.where(gcol == GRID_W - 1, 0.0, pltpu.roll(xg, t - 1, 0))
            if d == 2:
                halo = jnp.where(first, 0.0, _dot(h_above, w_ref[:, g0:g0 + 128]))
                return jnp.concatenate([halo, xg[:t - GRID_W]], axis=0)
            halo = jnp.where(last, 0.0, _dot(h_below, w_ref[:, g0:g0 + 128]))
            return jnp.concatenate([xg[GRID_W:], halo], axis=0)
    else:
        def shifted(xg, d, g0):
            if d == 0:
                return jnp.where(row == 0, 0.0, pltpu.roll(xg, 1, 0))
            return jnp.where(row == t - 1, 0.0, pltpu.roll(xg, t - 1, 0))

    u = _shift_lerp(p_scr, mu_ref, shifted, segments)
    r, k, v = u[:, :w], u[:, w:2 * w], u[:, 2 * w:3 * w]
    lo_wa = u[:, 3 * w:3 * w + 128]
    lo_gv = u[:, 3 * w + 128:3 * w + 384]

    lp = LORA_PASSES
    logw = -np.exp(-0.5) * _sigmoid(w0_ref[...] + _mm(jnp.tanh(lo_wa), wup_ref[...], passes=lp))
    iclr = _sigmoid(a0_ref[...] + _mm(lo_wa, aup_ref[...], passes=lp))
    gate = _mm(_sigmoid(lo_gv), gup_ref[...], passes=lp)
    if has_vres:
        mix = _sigmoid(v0_ref[...] + _mm(lo_gv, vup_ref[...], passes=lp))
        v = v + (vfirst_ref[0].astype(F32) - v) * mix

    kk = k * kk_ref[...]
    kk = kk * lax.rsqrt(jnp.maximum(_head_sum(kk * kk), 1e-24))
    bonus = _head_sum(r * k * rk_ref[...]) * v

    for o_ref, val in ((r_o, r), (v_o, v), (kk_o, kk), (bonus_o, bonus), (gate_o, gate)):
        o_ref[0] = val.astype(o_ref.dtype)
    ka = ka_ref[...]
    for d, (lw_o, k_o, b_o) in enumerate(((lwf_o, kf_o, bf_o), (lwb_o, kb_o, bb_o))):
        a = iclr[:, d * w:(d + 1) * w]
        lw_o[0] = logw[:, d * w:(d + 1) * w]
        k_o[0] = (k * (1.0 + (a - 1.0) * ka)).astype(k_o.dtype)
        b_o[0] = (kk * a).astype(b_o.dtype)


def _proj_prep(x, g, shift, scale, consts, rope, vfirst, quad, tm):
    b, l, d = x.shape
    w = consts["k_k"].shape[-1]
    cols = consts["widths"]
    assert cols[1] == 3 * w and cols[2] % w == 0
    assert quad or tm == l
    has_vres = vfirst is not None
    parts = 4 if quad else 2
    segments = (_direction_segments(consts["n_rwkv"], parts, 0)
                + _direction_segments(consts["n_vres"], parts, consts["n_rwkv"]))
    kern = functools.partial(_proj_prep_kernel, quad=quad, rope=rope is not None, has_vres=has_vres,
                             segments=segments, width=w, cols=cols)
    bm = shift.shape[0]
    mod_map = (lambda i, j: (i, 0, 0)) if bm > 1 else (lambda i, j: (0, 0, 0))
    const = lambda arr: pl.BlockSpec(arr.shape, lambda i, j: (0,) * arr.ndim, pipeline_mode=pl.Buffered(1))
    tok = lambda width: pl.BlockSpec((1, tm, width), lambda i, j: (i, j, 0))
    rows = tm // GRID_W
    n_rows = l // GRID_W
    in_specs = [tok(d)]
    args = [x]
    if quad:
        in_specs += [
            pl.BlockSpec((1, GRID_W, d), lambda i, j: (i, jnp.maximum(j * rows - 1, 0), 0)),
            pl.BlockSpec((1, GRID_W, d), lambda i, j: (i, jnp.minimum((j + 1) * rows, n_rows - 1), 0)),
        ]
        args += [x, x]
    in_specs += [const(g), pl.BlockSpec((1, 1, d), mod_map), pl.BlockSpec((1, 1, d), mod_map),
                 const(consts["w_proj"])]
    args += [g, shift, scale, consts["w_proj"]]
    if rope is not None:
        in_specs += [pl.BlockSpec((tm, w), lambda i, j: (j, 0))] * 2
        args += list(rope)
    if has_vres:
        in_specs.append(tok(w))
        args.append(vfirst)
    for name in ("mu", "w0", "w_up", "a0", "a_up", "g_up", "v_up", "v0", "k_k", "k_a", "r_k"):
        in_specs.append(const(consts[name]))
        args.append(consts[name])
    act = lambda width: jax.ShapeDtypeStruct((b, l, width), ACT)
    f32 = lambda width: jax.ShapeDtypeStruct((b, l, width), F32)
    return pl.pallas_call(
        kern,
        grid=(b, l // tm),
        in_specs=in_specs,
        out_specs=[tok(cols[1]), tok(cols[2])] + [tok(w)] * 11,
        out_shape=[act(cols[1]), act(cols[2])] + [act(w)] * 5 + [f32(w)] * 2 + [act(w)] * 4,
        scratch_shapes=[pltpu.VMEM((tm, cols[0]), F32)],
        compiler_params=_params("arbitrary", "arbitrary"),
        name="proj_prep",
    )(*args)


def _block_diag(y, block=HEAD_DIM):
    blk = lax.broadcasted_iota(jnp.int32, y.shape, 1) // block
    return jnp.concatenate([jnp.where(blk == h, y, 0.0) for h in range(y.shape[1] // block)], axis=0)


def _head_diag_mask(width):
    r = lax.broadcasted_iota(jnp.int32, (width, width), 0) // HEAD_DIM
    c = lax.broadcasted_iota(jnp.int32, (width, width), 1) // HEAD_DIM
    return r == c


def _rwkv_stream(r, v, kk, lw, kd, bd, reverse, state, out):
    r, v, kk, kd, bd = (t.astype(F32) for t in (r, v, kk, kd, bd))
    c = r.shape[0]
    t_idx = lax.broadcasted_iota(jnp.int32, (c, c), 0)
    s_idx = lax.broadcasted_iota(jnp.int32, (c, c), 1)
    tri = jnp.where((s_idx >= t_idx) if reverse else (s_idx <= t_idx), 1.0, 0.0).astype(BF16)
    lc = _mm_exact_lhs(tri, lw)
    l_end = lc[0:1] if reverse else lc[c - 1:c]
    e_inv = jnp.exp(-lc)
    to_end = jnp.exp(l_end - lc)
    heads = RWKV_GROUP // HEAD_DIM
    row = lax.broadcasted_iota(jnp.int32, (c, heads * c), 0)
    col = lax.broadcasted_iota(jnp.int32, (c, heads * c), 1) % c
    return dict(
        state=state, out=out,
        v=v, r_t=r * jnp.exp(lc), a_t=-kk * jnp.exp(lc - lw), k_t=kd * e_inv, b_t=bd * e_inv,
        k_hat=kd * to_end, b_hat=bd * to_end, p_end=jnp.exp(l_end),
        before=(col > row) if reverse else (col < row),
        upto=(col >= row) if reverse else (col <= row),
        eye=jnp.where(col == row, 1.0, 0.0))


def _rwkv_stages(streams, state, passes):
    c = streams[0]["v"].shape[0]
    gw = RWKV_GROUP
    assert c == HEAD_DIM
    chains = [(k, p) for k in range(len(streams)) for p in range(streams[0]["v"].shape[1] // gw)]
    sl = lambda p: slice(p * gw, (p + 1) * gw)
    key = lambda ch: streams[ch[0]]["state"] + (ch[1],)
    diag = _head_diag_mask(gw)

    z, zs = {}, {}
    for ch in chains:
        st, p = streams[ch[0]], ch[1]
        z[ch] = jnp.concatenate([st["a_t"][:, sl(p)], st["r_t"][:, sl(p)]], axis=0)
        zs[ch] = _mm(z[ch], state[key(ch)], NT, passes["state"])
    yield
    l_ab, m_rb, l_ak, m_rk = {}, {}, {}, {}
    for ch in chains:
        st, p = streams[ch[0]], ch[1]
        w_bk = jnp.concatenate([_block_diag(st["b_t"][:, sl(p)]), _block_diag(st["k_t"][:, sl(p)])], axis=0)
        gram = _mm(z[ch], w_bk, NT, passes["gram"])
        nb = gram.shape[1] // 2
        l_ab[ch] = jnp.where(st["before"], gram[:c, :nb], 0.0)
        m_rb[ch] = jnp.where(st["upto"], gram[c:, :nb], 0.0)
        l_ak[ch] = jnp.where(st["before"], gram[:c, nb:], 0.0)
        m_rk[ch] = jnp.where(st["upto"], gram[c:, nb:], 0.0)
    yield

    inv = {ch: streams[ch[0]]["eye"] + l_ab[ch] for ch in chains}
    power = {ch: _mm(l_ab[ch], _block_diag(l_ab[ch], c), NN, passes["inverse"]) for ch in chains}
    yield
    kv = {ch: _mm(jnp.concatenate([l_ak[ch], m_rk[ch]], axis=0),
                  _block_diag(streams[ch[0]]["v"][:, sl(ch[1])]), NN, passes["kv"]) for ch in chains}
    yield
    for _ in range(int(np.log2(c)) - 2):
        for ch in chains:
            w_ip = jnp.concatenate([_block_diag(inv[ch], c), _block_diag(power[ch], c)], axis=1)
            prod = _mm(power[ch], w_ip, NN, passes["inverse"])
            nb = prod.shape[1] // 2
            inv[ch] = inv[ch] + prod[:, :nb]
            power[ch] = prod[:, nb:]
        yield
    for ch in chains:
        inv[ch] = inv[ch] + _mm(power[ch], _block_diag(inv[ch], c), NN, passes["inverse"])
    yield

    u = {ch: _mm(inv[ch], _block_diag(zs[ch][:c] + kv[ch][:c]), NN, passes["solve"]) for ch in chains}
    yield
    for ch in chains:
        st, p = streams[ch[0]], ch[1]
        y_ref, g, rows = st["out"]
        y = zs[ch][c:] + kv[ch][c:] + _mm(m_rb[ch], _block_diag(u[ch]), NN, passes["out"])
        y_ref[g, rows, sl(p)] = y.astype(y_ref.dtype)
    yield
    for ch in chains:
        st, p = streams[ch[0]], ch[1]
        uv_t = jnp.concatenate([u[ch], st["v"][:, sl(p)]], axis=0).T
        upd = _mm(uv_t, jnp.concatenate([st["b_hat"][:, sl(p)], st["k_hat"][:, sl(p)]], axis=0), NN,
                  passes["update"])
        state[key(ch)] = state[key(ch)] * st["p_end"][:, sl(p)] + jnp.where(diag, upd, 0.0)
    yield


def _rope(x, cos, sin_signed):
    w = x.shape[1]
    half = HEAD_DIM // 2
    lane = lax.broadcasted_iota(jnp.int32, x.shape, 1) % HEAD_DIM
    rot = jnp.where(lane < half, pltpu.roll(x, w - half, 1), pltpu.roll(x, half, 1))
    return x * cos + rot * sin_signed


def _ret_stages(streams, s_scr):
    n_pairs = streams[0]["v"].shape[1] // PAIR
    chains = [(k, p) for k in range(len(streams)) for p in range(n_pairs)]
    sl = lambda p: slice(p * PAIR, (p + 1) * PAIR)
    diag = _head_diag_mask(PAIR)
    scores, cross, s0 = {}, {}, {}
    for ch in chains:
        st, p = streams[ch[0]], ch[1]
        s0[ch] = s_scr[st["state"] + (p,)]
        scores[ch] = _mm(st["q"][:, sl(p)], _block_diag(st["k"][:, sl(p)]), NT) * st["inner"][p]
        cross[ch] = _mm(st["q"][:, sl(p)], s0[ch]) * st["qdec"][:, sl(p)]
    yield
    for ch in chains:
        st, p = streams[ch[0]], ch[1]
        o_ref, g = st["out"]
        o = _mm(scores[ch], _block_diag(st["v"][:, sl(p)])) + cross[ch]
        o_ref[g, :, sl(p)] = o.astype(o_ref.dtype)
    yield
    for ch in chains:
        st, p = streams[ch[0]], ch[1]
        upd = _mm((st["k"][:, sl(p)] * st["kdec"][:, sl(p)]).T, st["v"][:, sl(p)])
        s_scr[st["state"] + (p,)] = s0[ch] * st["cdec"][:, sl(p)] + jnp.where(diag, upd, 0.0)
    yield


def _interleave(main, filler, every):
    n = 0
    for _ in main:
        n += 1
        if n % every == 0:
            next(filler, None)
    for _ in filler:
        pass


def _mix_scan_kernel(rf, vf, kkf, lwf, kf, bf, rb, vb, kkb, lwb, kb, bb, qkvf, qkvb, inf_ref, inb_ref,
                     dec_ref, s0_ref, t0_ref, yf_ref, yb_ref, of_ref, ob_ref, sout_ref, tout_ref,
                     s_scr, t_scr, *, passes, width):
    i = pl.program_id(1)

    @pl.when(i == 0)
    def _():
        s_scr[...] = s0_ref[...]
        t_scr[...] = t0_ref[...]

    c = RWKV_CHUNK
    w = width
    n_sub = rf.shape[1] // c
    n_groups = w // RWKV_GROUP

    def rwkv_all():
        state = {(d, g, p): s_scr[d, g, p] for d in range(2) for g in range(rf.shape[0]) for p in range(n_groups)}
        for sub in range(n_sub):
            streams = []
            for g in range(rf.shape[0]):
                fr = slice(sub * c, (sub + 1) * c)
                br = slice((n_sub - 1 - sub) * c, (n_sub - sub) * c)
                streams.append(_rwkv_stream(rf[g, fr], vf[g, fr], kkf[g, fr], lwf[g, fr], kf[g, fr], bf[g, fr],
                                            False, (0, g), (yf_ref, g, fr)))
                streams.append(_rwkv_stream(rb[g, br], vb[g, br], kkb[g, br], lwb[g, br], kb[g, br], bb[g, br],
                                            True, (1, g), (yb_ref, g, br)))
            yield from _rwkv_stages(streams, state, passes)
        for key, val in state.items():
            s_scr[key] = val

    ret_streams = []
    for g in range(qkvf.shape[0]):
        for d, (qkv, inner_ref, o_ref) in enumerate(((qkvf, inf_ref, of_ref), (qkvb, inb_ref, ob_ref))):
            x = qkv[g]
            ret_streams.append(dict(
                q=x[:, :w], k=x[:, w:2 * w], v=x[:, 2 * w:], inner=inner_ref, qdec=dec_ref[d, 0],
                kdec=dec_ref[d, 1], cdec=dec_ref[d, 2, 0:1], state=(d, g), out=(o_ref, g)))
    _interleave(rwkv_all(), _ret_stages(ret_streams, t_scr), every=RET_STAGE_EVERY)

    @pl.when(i == pl.num_programs(1) - 1)
    def _():
        sout_ref[...] = s_scr[...]
        tout_ref[...] = t_scr[...]


def _mix_scan(r, v, kk, lwf, lwb, kf, kb, bf, bb, qkv, tables, s0, t0):
    b, l, w = r.shape
    c = RET_CHUNK
    assert c % RWKV_CHUNK == 0
    nc = l // c
    g = SCAN_BATCH if b % SCAN_BATCH == 0 else 1
    inner_f, inner_b, dec = tables
    fwd = lambda width: pl.BlockSpec((g, c, width), lambda i, j: (i, j, 0))
    bwd = lambda width: pl.BlockSpec((g, c, width), lambda i, j: (i, nc - 1 - j, 0))
    st = lambda arr, **kw: pl.BlockSpec((2, g) + arr.shape[2:], lambda i, j: (0, i, 0, 0, 0), **kw)
    once = dict(pipeline_mode=pl.Buffered(1))
    const = lambda arr: pl.BlockSpec(arr.shape, lambda i, j: (0,) * arr.ndim, **once)
    f32 = lambda shape: jax.ShapeDtypeStruct(shape, F32)
    act = jax.ShapeDtypeStruct((b, l, w), ACT)
    return pl.pallas_call(
        functools.partial(_mix_scan_kernel, passes=dict(RWKV_PASSES), width=w),
        grid=(b // g, nc),
        in_specs=[fwd(w)] * 6 + [bwd(w)] * 6 + [fwd(3 * w), bwd(3 * w), const(inner_f), const(inner_b), const(dec),
                                                st(s0, **once), st(t0, **once)],
        out_specs=[fwd(w), bwd(w), fwd(w), bwd(w), st(s0), st(t0)],
        out_shape=[act] * 4 + [f32(s0.shape), f32(t0.shape)],
        scratch_shapes=[pltpu.VMEM((2, g) + s0.shape[2:], F32), pltpu.VMEM((2, g) + t0.shape[2:], F32)],
        compiler_params=_params("arbitrary", "arbitrary"),
        name="mix_scan",
    )(r, v, kk, lwf, kf, bf, r, v, kk, lwb, kb, bb, qkv, qkv, inner_f, inner_b, dec, s0, t0)


def _ret_tables(n_heads):
    c = RET_CHUNK
    lg = np.log(1.0 - 2.0 ** (-5.0 - np.arange(n_heads, dtype=np.float64)))
    idx = np.arange(c, dtype=np.float64)
    diff = idx[:, None] - idx[None, :]

    def inner(log_gamma, reverse):
        dd = -diff if reverse else diff
        m = np.where(dd >= 0, np.exp(log_gamma[:, None, None] * np.maximum(dd, 0.0)), 0.0)
        return m.reshape(n_heads // 2, 2, c, c).transpose(0, 2, 1, 3).reshape(n_heads // 2, c, 2 * c)

    def per_lane(tab):
        return np.repeat(tab, HEAD_DIM, axis=1)

    dec = []
    for reverse in (False, True):
        g = lg[::-1] if reverse else lg
        pos = (c - 1.0 - idx) if reverse else idx
        qd = per_lane(np.exp(g[None, :] * (pos[:, None] + 1.0)))
        kd = per_lane(np.exp(g[None, :] * (c - 1.0 - pos[:, None])))
        cd = per_lane(np.broadcast_to(np.exp(g * c)[None, :], (c, n_heads)))
        dec.append(np.stack([qd, kd, cd]))
    return (jnp.asarray(inner(lg, False), F32), jnp.asarray(inner(lg[::-1], True), F32),
            jnp.asarray(np.stack(dec), F32))


def _rope_tables(l, n_heads):
    rows = l // GRID_W
    row = jnp.repeat(jnp.arange(rows, dtype=F32), GRID_W)
    col = jnp.tile(jnp.arange(GRID_W, dtype=F32), rows)
    nf = HEAD_DIM // 4
    freqs = ROPE_BASE ** (-jnp.arange(nf, dtype=F32) / nf)
    ang = jnp.concatenate([row[:, None] * freqs, col[:, None] * freqs], axis=-1)
    cos = jnp.tile(jnp.cos(ang), (1, 2 * n_heads))
    sin = jnp.tile(jnp.concatenate([-jnp.sin(ang), jnp.sin(ang)], axis=-1), (1, n_heads))
    return cos, sin


def _group_norm(y, w, b, eps):
    inv_n = 1.0 / HEAD_DIM
    d = y - _head_sum(y) * inv_n
    return d * lax.rsqrt(_head_sum(d * d) * inv_n + eps) * w + b


def _out_ffn_kernel(x_ref, yf_ref, yb_ref, bonus_ref, gate_ref, of_ref, ob_ref, pg_ref,
                    lnw_ref, lnb_ref, gnw_ref, gnb_ref, wo_ref,
                    g1_ref, sh2_ref, sc2_ref, g2_ref, n2_ref, nf_ref, wfi_ref, wfo_ref,
                    o_ref, mix_scr, *, width, hidden, chunks, final_norm):
    t = pl.program_id(0)
    w = width

    @pl.when(t == 0)
    def _():
        mix_scr[...] = jnp.zeros(mix_scr.shape, mix_scr.dtype)

    slot = lax.rem(t, 2)
    attn = _dot(mix_scr[1 - slot], wo_ref[...])

    y_sum = yf_ref[0].astype(F32) + yb_ref[0].astype(F32)
    o_rwkv = (_group_norm(y_sum, lnw_ref[...], lnb_ref[...], RWKV_GN_EPS)
              + bonus_ref[0]) * gate_ref[0]
    pg = pg_ref[0].astype(F32)
    o_ret = (_group_norm(of_ref[0].astype(F32), gnw_ref[...], gnb_ref[...], RET_GN_EPS) * _silu(pg[:, :w])
             + _group_norm(ob_ref[0].astype(F32), gnw_ref[...], gnb_ref[...], RET_GN_EPS) * _silu(pg[:, w:]))
    mix_scr[slot, :, :w] = o_rwkv.astype(mix_scr.dtype)
    mix_scr[slot, :, w:] = o_ret.astype(mix_scr.dtype)

    x1 = x_ref[0] + g1_ref[0] * attn
    h = _modnorm(x1, n2_ref[...], sh2_ref[0], sc2_ref[0]).astype(BF16)
    acc = None
    for c0, c1 in chunks:
        gate = _dot(h, wfi_ref[:, c0:c1])
        up = _dot(h, wfi_ref[:, hidden + c0:hidden + c1])
        part = _dot((_silu(gate) * up).astype(BF16), wfo_ref[c0:c1])
        acc = part if acc is None else acc + part
    x2 = x1 + g2_ref[0] * acc
    if final_norm:
        ms = jnp.mean(x2 * x2, axis=-1, keepdims=True)
        x2 = (x2 * lax.rsqrt(ms + NORM_EPS)) * nf_ref[...]
    o_ref[0] = x2


def _out_ffn(x, yf, yb, bonus, gate, of, ob, pg, consts, mods, final_norm, tm):
    b, l, d = x.shape
    w = yf.shape[-1]
    hidden = consts["w_ffn_out"].shape[0]
    n_tiles = hidden // MXU_TILE
    n_chunks = -(-n_tiles // FFN_CHUNK_TILES)
    bounds = [MXU_TILE * (n_tiles * i // n_chunks) for i in range(n_chunks + 1)]
    bounds[-1] = hidden
    chunks = tuple(zip(bounds[:-1], bounds[1:]))
    bm = mods[0].shape[0]
    nj = l // tm
    n_tok_tiles = b * nj
    cur = lambda t: jnp.maximum(t - 1, 0)
    nxt = lambda t: jnp.minimum(t, n_tok_tiles - 1)
    mod_map = (lambda t: (cur(t) // nj, 0, 0)) if bm > 1 else (lambda t: (0, 0, 0))
    tok_cur = lambda width: pl.BlockSpec((1, tm, width), lambda t: (cur(t) // nj, cur(t) % nj, 0))
    tok_nxt = lambda width: pl.BlockSpec((1, tm, width), lambda t: (nxt(t) // nj, nxt(t) % nj, 0))
    const = lambda arr: pl.BlockSpec(arr.shape, lambda t: (0,) * arr.ndim, pipeline_mode=pl.Buffered(1))
    names = ("ln_w", "ln_b", "gn_w", "gn_b", "w_out")
    tail = ("norm2", "norm_f", "w_ffn_in", "w_ffn_out")
    in_specs = ([tok_cur(d)] + [tok_nxt(w)] * 6 + [tok_nxt(2 * w)] + [const(consts[n]) for n in names]
                + [pl.BlockSpec((1, 1, d), mod_map)] * 4 + [const(consts[n]) for n in tail])
    args = ([x, yf, yb, bonus, gate, of, ob, pg] + [consts[n] for n in names] + list(mods)
            + [consts[n] for n in tail])
    kern = functools.partial(_out_ffn_kernel, width=w, hidden=hidden, chunks=chunks, final_norm=final_norm)
    return pl.pallas_call(
        kern,
        grid=(n_tok_tiles + 1,),
        in_specs=in_specs,
        out_specs=tok_cur(d),
        out_shape=jax.ShapeDtypeStruct((b, l, d), F32),
        scratch_shapes=[pltpu.VMEM((2, tm, 2 * w), BF16)],
        compiler_params=_params("arbitrary"),
        name="out_ffn",
    )(*args)


def _pad_rows(a, rows, offset=0):
    out = jnp.zeros((rows,) + a.shape[1:], a.dtype)
    return out.at[offset:offset + a.shape[0]].set(a)


def _layer_consts(i, w_in, w_vres_down, mu_rwkv, mu_vres, w0, w_up, a0, a_up, g_up, k_k, k_a, r_k, v0, v_up,
                  ln_x_w, ln_x_b, ret_gn_w, ret_gn_b, w_out, w_ffn_in, w_ffn_out, norm2, norm_f):
    width = w0.shape[-1]
    n_decay, n_iclr, n_gate = w_up.shape[2], a_up.shape[2], g_up.shape[1]
    n_vres = v_up.shape[1]
    n_rwkv = 3 * width + n_decay + n_iclr + n_gate
    assert n_decay + n_iclr == 128 and n_gate + n_vres <= 256
    rwkv_cols = 3 * width + 128 + 256
    d = w_in.shape[1]
    has_vres = i > 0
    wi = w_in[i]
    pad = jnp.zeros((d, rwkv_cols - n_rwkv - n_vres), F32)
    vres_cols = w_vres_down[i - 1] if has_vres else jnp.zeros((d, n_vres), F32)
    w_proj = jnp.concatenate([wi[:, :n_rwkv], vres_cols, pad, wi[:, n_rwkv:]], axis=1).astype(BF16)
    mu_v = mu_vres[i - 1] if has_vres else jnp.zeros((n_vres,), F32)
    mu = jnp.concatenate([mu_rwkv[i], mu_v, jnp.zeros((rwkv_cols - n_rwkv - n_vres,), F32)])[None]
    cat2 = lambda a: jnp.concatenate([a[0], a[1]], axis=-1)
    row = lambda a: a.reshape(1, -1)
    return dict(
        w_proj=w_proj, widths=(rwkv_cols, 3 * width, 2 * width), n_rwkv=n_rwkv, n_vres=n_vres, mu=mu,
        w0=row(cat2(w0[i])), w_up=_pad_rows(cat2(w_up[i]), 128), a0=row(cat2(a0[i])),
        a_up=_pad_rows(cat2(a_up[i]), 128, n_decay), g_up=_pad_rows(g_up[i], 256),
        v_up=_pad_rows(v_up[i - 1] if has_vres else jnp.zeros((n_vres, width), F32), 256, n_gate),
        v0=row(v0[i - 1] if has_vres else jnp.zeros((width,), F32)),
        k_k=row(k_k[i]), k_a=row(k_a[i]), r_k=row(r_k[i]),
        ln_w=row(ln_x_w[i]), ln_b=row(ln_x_b[i]), gn_w=row(ret_gn_w[i]), gn_b=row(ret_gn_b[i]),
        w_out=w_out[i].astype(BF16), w_ffn_in=w_ffn_in[i].astype(BF16), w_ffn_out=w_ffn_out[i].astype(BF16),
        norm2=row(norm2[i]), norm_f=row(norm_f),
    )


def _mixers(x, norm1, shift, scale, consts, vfirst, s_rwkv, s_ret, rope, tables, quad, tm):
    p_qkv, p_g, r, v, kk, bonus, gate, lwf, lwb, kf, kb, bf, bb = _proj_prep(
        x, norm1, shift, scale, consts, rope, vfirst, quad, tm)
    yf, yb, of, ob, s_rwkv, s_ret = _mix_scan(r, v, kk, lwf, lwb, kf, kb, bf, bb, p_qkv, tables, s_rwkv, s_ret)
    return (yf, yb, bonus, gate, of, ob, p_g), v, s_rwkv, s_ret


def kernel(x, c, ctx, c_ctx, w_ada, b_ada, norm1, norm2, norm_f, w_in, w_vres_down, mu_rwkv, mu_vres, w0, w_up, a0, a_up, g_up, k_k, k_a, r_k, v0, v_up, ln_x_w, ln_x_b, ret_gn_w, ret_gn_b, w_out, w_ffn_in, w_ffn_out):
    b, l, d = x.shape
    l_ctx = ctx.shape[1]
    depth = w_in.shape[0]
    width = w0.shape[-1]
    n_heads = width // HEAD_DIM
    n_pairs = n_heads // 2
    tm = min(TM_PROJ, l)
    tm_ctx = l_ctx
    tm_ffn = min(TM_FFN, l)
    tm_ffn_ctx = min(TM_FFN, l_ctx)

    m_pad = -(-(b + 1) // 8) * 8
    cond = jnp.zeros((m_pad, d), F32).at[:b].set(c).at[b].set(c_ctx)
    mod = _ada(cond, w_ada, b_ada)

    tables = _ret_tables(n_heads)
    rope_l = _rope_tables(l, n_heads)
    rope_c = None
    zero_ret = jnp.zeros((2, b, width // PAIR, PAIR, PAIR), F32)
    zero_rwkv = jnp.zeros((2, b, width // RWKV_GROUP, RWKV_GROUP, RWKV_GROUP), F32)

    vfirst_c = vfirst_l = None
    for i in range(depth):
        last = i == depth - 1
        consts = _layer_consts(i, w_in, w_vres_down, mu_rwkv, mu_vres, w0, w_up, a0, a_up, g_up, k_k, k_a, r_k,
                               v0, v_up, ln_x_w, ln_x_b, ret_gn_w, ret_gn_b, w_out, w_ffn_in, w_ffn_out,
                               norm2, norm_f)
        mods_l = [mod[i, :b, None, j * d:(j + 1) * d] for j in range(6)]
        mods_c = [mod[i, b:b + 1, None, j * d:(j + 1) * d] for j in range(6)]
        n1 = norm1[i][None]

        mix_c, v_c, s_rwkv, s_ret = _mixers(ctx, n1, mods_c[0], mods_c[1], consts, vfirst_c,
                                            zero_rwkv, zero_ret, rope_c, tables, False, tm_ctx)
        mix_l, v_l, _, _ = _mixers(x, n1, mods_l[0], mods_l[1], consts, vfirst_l,
                                   s_rwkv, s_ret, rope_l, tables, True, tm)
        if i == 0:
            vfirst_c, vfirst_l = v_c, v_l
        x = _out_ffn(x, *mix_l, consts, (mods_l[2], mods_l[3], mods_l[4], mods_l[5]), last, tm_ffn)
        if not last:
            ctx = _out_ffn(ctx, *mix_c, consts, (mods_c[2], mods_c[3], mods_c[4], mods_c[5]), False, tm_ffn_ctx)
    return x
```

```python
import functools

import numpy as np
import jax
import jax.numpy as jnp
from jax import lax
from jax.experimental import pallas as pl
from jax.experimental.pallas import tpu as pltpu

F32 = jnp.float32
BF16 = jnp.bfloat16
ACT = BF16

GRID_W = 64
HEAD_DIM = 64
PAIR = 2 * HEAD_DIM
RWKV_GROUP = 2 * HEAD_DIM
RWKV_CHUNK = 64
RET_CHUNK = 128
ROPE_BASE = 10000.0
NORM_EPS = 1e-6
RWKV_GN_EPS = 64e-5
RET_GN_EPS = 1e-5
VMEM_LIMIT = 60 * 1024 * 1024
TM_PROJ = 512
TM_FFN = 512
MXU_TILE = 256
FFN_CHUNK_TILES = 3
SCAN_BATCH = 4
RET_STAGE_EVERY = 7
LORA_PASSES = 1
RWKV_PASSES = dict(state=1, gram=1, inverse=1, kv=1, solve=1, residual=3, out=1, update=1)

NN = ((1,), (0,))
NT = ((1,), (1,))


def _dot(a, b, dims=NN):
    return lax.dot_general(a, b, (dims, ((), ())), preferred_element_type=F32)


def _split(a):
    hi = a.astype(BF16)
    lo = (a - hi.astype(F32)).astype(BF16)
    return hi, lo


def _mm(a, b, dims=NN, passes=1):
    if passes == 1:
        return _dot(a.astype(BF16), b.astype(BF16), dims)
    ah, al = _split(a)
    bh, bl = _split(b)
    lhs = jnp.concatenate([ah, al, ah], axis=1)
    rhs = jnp.concatenate([bh, bh, bl], axis=0 if dims == NN else 1)
    return _dot(lhs, rhs, dims)


def _mm_exact_lhs(a_bf16, b, pieces=3):
    acc = None
    rem = b
    for _ in range(pieces):
        p = rem.astype(BF16)
        t = _dot(a_bf16, p)
        acc = t if acc is None else acc + t
        rem = rem - p.astype(F32)
    return acc


def _sigmoid(x):
    return 1.0 / (1.0 + jnp.exp(-x))


def _silu(x):
    return x * _sigmoid(x)


def _params(*sem):
    return pltpu.CompilerParams(dimension_semantics=sem, vmem_limit_bytes=VMEM_LIMIT)


def _ada_kernel(c_ref, w_ref, b_ref, o_ref):
    o_ref[0] = _mm(_silu(c_ref[...]), w_ref[0], passes=3) + b_ref[0]


def _ada(cond, w_ada, b_ada):
    depth, d, n = w_ada.shape
    m = cond.shape[0]
    tn = n // 4
    return pl.pallas_call(
        _ada_kernel,
        grid=(depth, n // tn),
        in_specs=[pl.BlockSpec((m, d), lambda i, j: (0, 0)),
                  pl.BlockSpec((1, d, tn), lambda i, j: (i, 0, j)),
                  pl.BlockSpec((1, 1, tn), lambda i, j: (i, 0, j))],
        out_specs=pl.BlockSpec((1, m, tn), lambda i, j: (i, 0, j)),
        out_shape=jax.ShapeDtypeStruct((depth, m, n), F32),
        compiler_params=_params("arbitrary", "arbitrary"),
        name="ada",
    )(cond, w_ada, b_ada.reshape(depth, 1, n))


def _modnorm(x, g, shift, scale):
    ms = jnp.mean(x * x, axis=-1, keepdims=True)
    y = (x * lax.rsqrt(ms + NORM_EPS)) * g
    return y * (1.0 + scale) + shift


def _direction_segments(n_cols, parts, offset):
    q = n_cols // parts
    return [(offset + i * q, offset + (i + 1) * q, i) for i in range(parts)]


def _shift_lerp(p_ref, mu_ref, shifted_fn, segments):
    outs = []
    for g0 in range(0, p_ref.shape[1], 128):
        xg = p_ref[:, g0:g0 + 128]
        lane = lax.broadcasted_iota(jnp.int32, xg.shape, 1) + g0
        s = None
        for (c0, c1, d) in segments:
            lo, hi = max(c0, g0), min(c1, g0 + 128)
            if lo >= hi:
                continue
            sd = shifted_fn(xg, d, g0)
            if lo == g0 and hi == g0 + 128:
                s = sd
            else:
                part = jnp.where((lane >= lo) & (lane < hi), sd, 0.0)
                s = part if s is None else s + part
        if s is None:
            s = jnp.zeros_like(xg)
        outs.append(xg + (s - xg) * mu_ref[:, g0:g0 + 128])
    return jnp.concatenate(outs, axis=1)


def _head_sum(x):
    outs = []
    for g0 in range(0, x.shape[1], PAIR):
        xg = x[:, g0:g0 + PAIR]
        first = lax.broadcasted_iota(jnp.int32, xg.shape, 1) < HEAD_DIM
        s_first = jnp.sum(jnp.where(first, xg, 0.0), axis=1, keepdims=True)
        s_second = jnp.sum(jnp.where(first, 0.0, xg), axis=1, keepdims=True)
        outs.append(jnp.where(first, s_first, s_second))
    return jnp.concatenate(outs, axis=1)


def _proj_prep_kernel(*refs, quad, rope, has_vres, segments, width, cols):
    it = iter(refs)
    x_ref = next(it)
    xp_ref = next(it) if quad else None
    xn_ref = next(it) if quad else None
    g_ref, sh_ref, sc_ref, w_ref = (next(it) for _ in range(4))
    cos_ref = next(it) if rope else None
    sin_ref = next(it) if rope else None
    vfirst_ref = next(it) if has_vres else None
    (mu_ref, w0_ref, wup_ref, a0_ref, aup_ref, gup_ref, vup_ref, v0_ref, kk_ref, ka_ref,
     rk_ref) = (next(it) for _ in range(11))
    qkv_o, pg_o = next(it), next(it)
    (r_o, v_o, kk_o, bonus_o, gate_o, lwf_o, lwb_o, kf_o, kb_o, bf_o, bb_o) = (next(it) for _ in range(11))
    p_scr = next(it)
    n_rwkv, n_qkv, n_g = cols
    w = width

    g, sh, sc = g_ref[...], sh_ref[0], sc_ref[0]
    h = _modnorm(x_ref[0], g, sh, sc).astype(BF16)
    for c0 in range(0, n_rwkv, w):
        c1 = min(c0 + w, n_rwkv)
        p_scr[:, c0:c1] = _dot(h, w_ref[:, c0:c1])
    for j in range(3):
        y = _dot(h, w_ref[:, n_rwkv + j * w:n_rwkv + (j + 1) * w])
        if rope and j < 2:
            y = _rope(y, cos_ref[...], sin_ref[...])
        if j == 1:
            y = y * HEAD_DIM ** -0.5
        qkv_o[0, :, j * w:(j + 1) * w] = y.astype(qkv_o.dtype)
    for c0 in range(0, n_g, w):
        pg_o[0, :, c0:c0 + w] = _dot(h, w_ref[:, n_rwkv + n_qkv + c0:n_rwkv + n_qkv + c0 + w]).astype(pg_o.dtype)

    t = x_ref.shape[1]
    row = lax.broadcasted_iota(jnp.int32, (t, 128), 0)
    if quad:
        gcol = row % GRID_W
        first = pl.program_id(1) == 0
        last = pl.program_id(1) == pl.num_programs(1) - 1
        h_above = _modnorm(xp_ref[0], g, sh, sc).astype(BF16)
        h_below = _modnorm(xn_ref[0], g, sh, sc).astype(BF16)

        def shifted(xg, d, g0):
            if d == 0:
                return jnp.where(gcol == 0, 0.0, pltpu.roll(xg, 1, 0))
            if d == 1:
                return jnp.where(gcol == GRID_W - 1, 0.0, pltpu.roll(xg, t - 1, 0))
            if d == 2:
                halo = jnp.where(first, 0.0, _dot(h_above, w_ref[:, g0:g0 + 128]))
                return jnp.concatenate([halo, xg[:t - GRID_W]], axis=0)
            halo = jnp.where(last, 0.0, _dot(h_below, w_ref[:, g0:g0 + 128]))
            return jnp.concatenate([xg[GRID_W:], halo], axis=0)
    else:
        def shifted(xg, d, g0):
            if d == 0:
                return jnp.where(row == 0, 0.0, pltpu.roll(xg, 1, 0))
            return jnp.where(row == t - 1, 0.0, pltpu.roll(xg, t - 1, 0))

    u = _shift_lerp(p_scr, mu_ref, shifted, segments)
    r, k, v = u[:, :w], u[:, w:2 * w], u[:, 2 * w:3 * w]
    lo_wa = u[:, 3 * w:3 * w + 128]
    lo_gv = u[:, 3 * w + 128:3 * w + 384]

    lp = LORA_PASSES
    logw = -np.exp(-0.5) * _sigmoid(w0_ref[...] + _mm(jnp.tanh(lo_wa), wup_ref[...], passes=lp))
    iclr = _sigmoid(a0_ref[...] + _mm(lo_wa, aup_ref[...], passes=lp))
    gate = _mm(_sigmoid(lo_gv), gup_ref[...], passes=lp)
    if has_vres:
        mix = _sigmoid(v0_ref[...] + _mm(lo_gv, vup_ref[...], passes=lp))
        v = v + (vfirst_ref[0].astype(F32) - v) * mix

    kk = k * kk_ref[...]
    kk = kk * lax.rsqrt(jnp.maximum(_head_sum(kk * kk), 1e-24))
    bonus = _head_sum(r * k * rk_ref[...]) * v

    for o_ref, val in ((r_o, r), (v_o, v), (kk_o, kk), (bonus_o, bonus), (gate_o, gate)):
        o_ref[0] = val.astype(o_ref.dtype)
    ka = ka_ref[...]
    for d, (lw_o, k_o, b_o) in enumerate(((lwf_o, kf_o, bf_o), (lwb_o, kb_o, bb_o))):
        a = iclr[:, d * w:(d + 1) * w]
        lw_o[0] = logw[:, d * w:(d + 1) * w]
        k_o[0] = (k * (1.0 + (a - 1.0) * ka)).astype(k_o.dtype)
        b_o[0] = (kk * a).astype(b_o.dtype)


def _proj_prep(x, g, shift, scale, consts, rope, vfirst, quad, tm):
    b, l, d = x.shape
    w = consts["k_k"].shape[-1]
    cols = consts["widths"]
    assert cols[1] == 3 * w and cols[2] % w == 0
    assert quad or tm == l
    has_vres = vfirst is not None
    parts = 4 if quad else 2
    segments = (_direction_segments(consts["n_rwkv"], parts, 0)
                + _direction_segments(consts["n_vres"], parts, consts["n_rwkv"]))
    kern = functools.partial(_proj_prep_kernel, quad=quad, rope=rope is not None, has_vres=has_vres,
                             segments=segments, width=w, cols=cols)
    bm = shift.shape[0]
    mod_map = (lambda i, j: (i, 0, 0)) if bm > 1 else (lambda i, j: (0, 0, 0))
    const = lambda arr: pl.BlockSpec(arr.shape, lambda i, j: (0,) * arr.ndim, pipeline_mode=pl.Buffered(1))
    tok = lambda width: pl.BlockSpec((1, tm, width), lambda i, j: (i, j, 0))
    rows = tm // GRID_W
    n_rows = l // GRID_W
    in_specs = [tok(d)]
    args = [x]
    if quad:
        in_specs += [
            pl.BlockSpec((1, GRID_W, d), lambda i, j: (i, jnp.maximum(j * rows - 1, 0), 0)),
            pl.BlockSpec((1, GRID_W, d), lambda i, j: (i, jnp.minimum((j + 1) * rows, n_rows - 1), 0)),
        ]
        args += [x, x]
    in_specs += [const(g), pl.BlockSpec((1, 1, d), mod_map), pl.BlockSpec((1, 1, d), mod_map),
                 const(consts["w_proj"])]
    args += [g, shift, scale, consts["w_proj"]]
    if rope is not None:
        in_specs += [pl.BlockSpec((tm, w), lambda i, j: (j, 0))] * 2
        args += list(rope)
    if has_vres:
        in_specs.append(tok(w))
        args.append(vfirst)
    for name in ("mu", "w0", "w_up", "a0", "a_up", "g_up", "v_up", "v0", "k_k", "k_a", "r_k"):
        in_specs.append(const(consts[name]))
        args.append(consts[name])
    act = lambda width: jax.ShapeDtypeStruct((b, l, width), ACT)
    f32 = lambda width: jax.ShapeDtypeStruct((b, l, width), F32)
    return pl.pallas_call(
        kern,
        grid=(b, l // tm),
        in_specs=in_specs,
        out_specs=[tok(cols[1]), tok(cols[2])] + [tok(w)] * 11,
        out_shape=[act(cols[1]), act(cols[2])] + [act(w)] * 5 + [f32(w)] * 2 + [act(w)] * 4,
        scratch_shapes=[pltpu.VMEM((tm, cols[0]), F32)],
        compiler_params=_params("arbitrary", "arbitrary"),
        name="proj_prep",
    )(*args)


def _block_diag(y, block=HEAD_DIM):
    blk = lax.broadcasted_iota(jnp.int32, y.shape, 1) // block
    return jnp.concatenate([jnp.where(blk == h, y, 0.0) for h in range(y.shape[1] // block)], axis=0)


def _head_diag_mask(width):
    r = lax.broadcasted_iota(jnp.int32, (width, width), 0) // HEAD_DIM
    c = lax.broadcasted_iota(jnp.int32, (width, width), 1) // HEAD_DIM
    return r == c


def _rwkv_stream(r, v, kk, lw, kd, bd, reverse, state, out):
    r, v, kk, kd, bd = (t.astype(F32) for t in (r, v, kk, kd, bd))
    c = r.shape[0]
    t_idx = lax.broadcasted_iota(jnp.int32, (c, c), 0)
    s_idx = lax.broadcasted_iota(jnp.int32, (c, c), 1)
    tri = jnp.where((s_idx >= t_idx) if reverse else (s_idx <= t_idx), 1.0, 0.0).astype(BF16)
    lc = _mm_exact_lhs(tri, lw)
    l_end = lc[0:1] if reverse else lc[c - 1:c]
    e_inv = jnp.exp(-lc)
    to_end = jnp.exp(l_end - lc)
    heads = RWKV_GROUP // HEAD_DIM
    row = lax.broadcasted_iota(jnp.int32, (c, heads * c), 0)
    col = lax.broadcasted_iota(jnp.int32, (c, heads * c), 1) % c
    return dict(
        state=state, out=out,
        v=v, r_t=r * jnp.exp(lc), a_t=-kk * jnp.exp(lc - lw), k_t=kd * e_inv, b_t=bd * e_inv,
        k_hat=kd * to_end, b_hat=bd * to_end, p_end=jnp.exp(l_end),
        before=(col > row) if reverse else (col < row),
        upto=(col >= row) if reverse else (col <= row),
        eye=jnp.where(col == row, 1.0, 0.0))


def _rwkv_stages(streams, state, passes):
    c = streams[0]["v"].shape[0]
    gw = RWKV_GROUP
    assert c == HEAD_DIM
    chains = [(k, p) for k in range(len(streams)) for p in range(streams[0]["v"].shape[1] // gw)]
    sl = lambda p: slice(p * gw, (p + 1) * gw)
    key = lambda ch: streams[ch[0]]["state"] + (ch[1],)
    diag = _head_diag_mask(gw)

    z, zs = {}, {}
    for ch in chains:
        st, p = streams[ch[0]], ch[1]
        z[ch] = jnp.concatenate([st["a_t"][:, sl(p)], st["r_t"][:, sl(p)]], axis=0)
        zs[ch] = _mm(z[ch], state[key(ch)], NT, passes["state"])
    yield
    l_ab, m_rb, l_ak, m_rk = {}, {}, {}, {}
    for ch in chains:
        st, p = streams[ch[0]], ch[1]
        w_bk = jnp.concatenate([_block_diag(st["b_t"][:, sl(p)]), _block_diag(st["k_t"][:, sl(p)])], axis=0)
        gram = _mm(z[ch], w_bk, NT, passes["gram"])
        nb = gram.shape[1] // 2
        l_ab[ch] = jnp.where(st["before"], gram[:c, :nb], 0.0)
        m_rb[ch] = jnp.where(st["upto"], gram[c:, :nb], 0.0)
        l_ak[ch] = jnp.where(st["before"], gram[:c, nb:], 0.0)
        m_rk[ch] = jnp.where(st["upto"], gram[c:, nb:], 0.0)
    yield

    inv = {ch: streams[ch[0]]["eye"] + l_ab[ch] for ch in chains}
    power = {ch: _mm(l_ab[ch], _block_diag(l_ab[ch], c), NN, passes["inverse"]) for ch in chains}
    yield
    kv = {ch: _mm(jnp.concatenate([l_ak[ch], m_rk[ch]], axis=0),
                  _block_diag(streams[ch[0]]["v"][:, sl(ch[1])]), NN, passes["kv"]) for ch in chains}
    yield
    for _ in range(int(np.log2(c)) - 2):
        for ch in chains:
            w_ip = jnp.concatenate([_block_diag(inv[ch], c), _block_diag(power[ch], c)], axis=1)
            prod = _mm(power[ch], w_ip, NN, passes["inverse"])
            nb = prod.shape[1] // 2
            inv[ch] = inv[ch] + prod[:, :nb]
            power[ch] = prod[:, nb:]
        yield
    for ch in chains:
        inv[ch] = inv[ch] + _mm(power[ch], _block_diag(inv[ch], c), NN, passes["inverse"])
    yield

    rhs = {ch: zs[ch][:c] + kv[ch][:c] for ch in chains}
    u = {ch: _mm(inv[ch], _block_diag(rhs[ch]), NN, passes["solve"]) for ch in chains}
    yield
    resid = {ch: rhs[ch] - u[ch] + _mm(l_ab[ch], _block_diag(u[ch]), NN, passes["residual"]) for ch in chains}
    yield
    u = {ch: u[ch] + _mm(inv[ch], _block_diag(resid[ch]), NN, passes["solve"]) for ch in chains}
    yield
    for ch in chains:
        st, p = streams[ch[0]], ch[1]
        y_ref, g, rows = st["out"]
        y = zs[ch][c:] + kv[ch][c:] + _mm(m_rb[ch], _block_diag(u[ch]), NN, passes["out"])
        y_ref[g, rows, sl(p)] = y.astype(y_ref.dtype)
    yield
    for ch in chains:
        st, p = streams[ch[0]], ch[1]
        uv_t = jnp.concatenate([u[ch], st["v"][:, sl(p)]], axis=0).T
        upd = _mm(uv_t, jnp.concatenate([st["b_hat"][:, sl(p)], st["k_hat"][:, sl(p)]], axis=0), NN,
                  passes["update"])
        state[key(ch)] = state[key(ch)] * st["p_end"][:, sl(p)] + jnp.where(diag, upd, 0.0)
    yield


def _rope(x, cos, sin_signed):
    w = x.shape[1]
    half = HEAD_DIM // 2
    lane = lax.broadcasted_iota(jnp.int32, x.shape, 1) % HEAD_DIM
    rot = jnp.where(lane < half, pltpu.roll(x, w - half, 1), pltpu.roll(x, half, 1))
    return x * cos + rot * sin_signed


def _ret_stages(streams, s_scr):
    n_pairs = streams[0]["v"].shape[1] // PAIR
    chains = [(k, p) for k in range(len(streams)) for p in range(n_pairs)]
    sl = lambda p: slice(p * PAIR, (p + 1) * PAIR)
    diag = _head_diag_mask(PAIR)
    scores, cross, s0 = {}, {}, {}
    for ch in chains:
        st, p = streams[ch[0]], ch[1]
        s0[ch] = s_scr[st["state"] + (p,)]
        scores[ch] = _mm(st["q"][:, sl(p)], _block_diag(st["k"][:, sl(p)]), NT) * st["inner"][p]
        cross[ch] = _mm(st["q"][:, sl(p)], s0[ch]) * st["qdec"][:, sl(p)]
    yield
    for ch in chains:
        st, p = streams[ch[0]], ch[1]
        o_ref, g = st["out"]
        o = _mm(scores[ch], _block_diag(st["v"][:, sl(p)])) + cross[ch]
        o_ref[g, :, sl(p)] = o.astype(o_ref.dtype)
    yield
    for ch in chains:
        st, p = streams[ch[0]], ch[1]
        upd = _mm((st["k"][:, sl(p)] * st["kdec"][:, sl(p)]).T, st["v"][:, sl(p)])
        s_scr[st["state"] + (p,)] = s0[ch] * st["cdec"][:, sl(p)] + jnp.where(diag, upd, 0.0)
    yield


def _interleave(main, filler, every):
    n = 0
    for _ in main:
        n += 1
        if n % every == 0:
            next(filler, None)
    for _ in filler:
        pass


def _mix_scan_kernel(rf, vf, kkf, lwf, kf, bf, rb, vb, kkb, lwb, kb, bb, qkvf, qkvb, inf_ref, inb_ref,
                     dec_ref, s0_ref, t0_ref, yf_ref, yb_ref, of_ref, ob_ref, sout_ref, tout_ref,
                     s_scr, t_scr, *, passes, width):
    i = pl.program_id(1)

    @pl.when(i == 0)
    def _():
        s_scr[...] = s0_ref[...]
        t_scr[...] = t0_ref[...]

    c = RWKV_CHUNK
    w = width
    n_sub = rf.shape[1] // c
    n_groups = w // RWKV_GROUP

    def rwkv_all():
        state = {(d, g, p): s_scr[d, g, p] for d in range(2) for g in range(rf.shape[0]) for p in range(n_groups)}
        for sub in range(n_sub):
            streams = []
            for g in range(rf.shape[0]):
                fr = slice(sub * c, (sub + 1) * c)
                br = slice((n_sub - 1 - sub) * c, (n_sub - sub) * c)
                streams.append(_rwkv_stream(rf[g, fr], vf[g, fr], kkf[g, fr], lwf[g, fr], kf[g, fr], bf[g, fr],
                                            False, (0, g), (yf_ref, g, fr)))
                streams.append(_rwkv_stream(rb[g, br], vb[g, br], kkb[g, br], lwb[g, br], kb[g, br], bb[g, br],
                                            True, (1, g), (yb_ref, g, br)))
            yield from _rwkv_stages(streams, state, passes)
        for key, val in state.items():
            s_scr[key] = val

    ret_streams = []
    for g in range(qkvf.shape[0]):
        for d, (qkv, inner_ref, o_ref) in enumerate(((qkvf, inf_ref, of_ref), (qkvb, inb_ref, ob_ref))):
            x = qkv[g]
            ret_streams.append(dict(
                q=x[:, :w], k=x[:, w:2 * w], v=x[:, 2 * w:], inner=inner_ref, qdec=dec_ref[d, 0],
                kdec=dec_ref[d, 1], cdec=dec_ref[d, 2, 0:1], state=(d, g), out=(o_ref, g)))
    _interleave(rwkv_all(), _ret_stages(ret_streams, t_scr), every=RET_STAGE_EVERY)

    @pl.when(i == pl.num_programs(1) - 1)
    def _():
        sout_ref[...] = s_scr[...]
        tout_ref[...] = t_scr[...]


def _mix_scan(r, v, kk, lwf, lwb, kf, kb, bf, bb, qkv, tables, s0, t0):
    b, l, w = r.shape
    c = RET_CHUNK
    assert c % RWKV_CHUNK == 0
    nc = l // c
    g = SCAN_BATCH if b % SCAN_BATCH == 0 else 1
    inner_f, inner_b, dec = tables
    fwd = lambda width: pl.BlockSpec((g, c, width), lambda i, j: (i, j, 0))
    bwd = lambda width: pl.BlockSpec((g, c, width), lambda i, j: (i, nc - 1 - j, 0))
    st = lambda arr, **kw: pl.BlockSpec((2, g) + arr.shape[2:], lambda i, j: (0, i, 0, 0, 0), **kw)
    once = dict(pipeline_mode=pl.Buffered(1))
    const = lambda arr: pl.BlockSpec(arr.shape, lambda i, j: (0,) * arr.ndim, **once)
    f32 = lambda shape: jax.ShapeDtypeStruct(shape, F32)
    act = jax.ShapeDtypeStruct((b, l, w), ACT)
    return pl.pallas_call(
        functools.partial(_mix_scan_kernel, passes=dict(RWKV_PASSES), width=w),
        grid=(b // g, nc),
        in_specs=[fwd(w)] * 6 + [bwd(w)] * 6 + [fwd(3 * w), bwd(3 * w), const(inner_f), const(inner_b), const(dec),
                                                st(s0, **once), st(t0, **once)],
        out_specs=[fwd(w), bwd(w), fwd(w), bwd(w), st(s0), st(t0)],
        out_shape=[act] * 4 + [f32(s0.shape), f32(t0.shape)],
        scratch_shapes=[pltpu.VMEM((2, g) + s0.shape[2:], F32), pltpu.VMEM((2, g) + t0.shape[2:], F32)],
        compiler_params=_params("arbitrary", "arbitrary"),
        name="mix_scan",
    )(r, v, kk, lwf, kf, bf, r, v, kk, lwb, kb, bb, qkv, qkv, inner_f, inner_b, dec, s0, t0)


def _ret_tables(n_heads):
    c = RET_CHUNK
    lg = np.log(1.0 - 2.0 ** (-5.0 - np.arange(n_heads, dtype=np.float64)))
    idx = np.arange(c, dtype=np.float64)
    diff = idx[:, None] - idx[None, :]

    def inner(log_gamma, reverse):
        dd = -diff if reverse else diff
        m = np.where(dd >= 0, np.exp(log_gamma[:, None, None] * np.maximum(dd, 0.0)), 0.0)
        return m.reshape(n_heads // 2, 2, c, c).transpose(0, 2, 1, 3).reshape(n_heads // 2, c, 2 * c)

    def per_lane(tab):
        return np.repeat(tab, HEAD_DIM, axis=1)

    dec = []
    for reverse in (False, True):
        g = lg[::-1] if reverse else lg
        pos = (c - 1.0 - idx) if reverse else idx
        qd = per_lane(np.exp(g[None, :] * (pos[:, None] + 1.0)))
        kd = per_lane(np.exp(g[None, :] * (c - 1.0 - pos[:, None])))
        cd = per_lane(np.broadcast_to(np.exp(g * c)[None, :], (c, n_heads)))
        dec.append(np.stack([qd, kd, cd]))
    return (jnp.asarray(inner(lg, False), F32), jnp.asarray(inner(lg[::-1], True), F32),
            jnp.asarray(np.stack(dec), F32))


def _rope_tables(l, n_heads):
    rows = l // GRID_W
    row = jnp.repeat(jnp.arange(rows, dtype=F32), GRID_W)
    col = jnp.tile(jnp.arange(GRID_W, dtype=F32), rows)
    nf = HEAD_DIM // 4
    freqs = ROPE_BASE ** (-jnp.arange(nf, dtype=F32) / nf)
    ang = jnp.concatenate([row[:, None] * freqs, col[:, None] * freqs], axis=-1)
    cos = jnp.tile(jnp.cos(ang), (1, 2 * n_heads))
    sin = jnp.tile(jnp.concatenate([-jnp.sin(ang), jnp.sin(ang)], axis=-1), (1, n_heads))
    return cos, sin


def _group_norm(y, w, b, eps):
    inv_n = 1.0 / HEAD_DIM
    d = y - _head_sum(y) * inv_n
    return d * lax.rsqrt(_head_sum(d * d) * inv_n + eps) * w + b


def _out_ffn_kernel(x_ref, yf_ref, yb_ref, bonus_ref, gate_ref, of_ref, ob_ref, pg_ref,
                    lnw_ref, lnb_ref, gnw_ref, gnb_ref, wo_ref,
                    g1_ref, sh2_ref, sc2_ref, g2_ref, n2_ref, nf_ref, wfi_ref, wfo_ref,
                    o_ref, mix_scr, *, width, hidden, chunks, final_norm):
    t = pl.program_id(0)
    w = width

    @pl.when(t == 0)
    def _():
        mix_scr[...] = jnp.zeros(mix_scr.shape, mix_scr.dtype)

    slot = lax.rem(t, 2)
    attn = _dot(mix_scr[1 - slot], wo_ref[...])

    y_sum = yf_ref[0].astype(F32) + yb_ref[0].astype(F32)
    o_rwkv = (_group_norm(y_sum, lnw_ref[...], lnb_ref[...], RWKV_GN_EPS)
              + bonus_ref[0]) * gate_ref[0]
    pg = pg_ref[0].astype(F32)
    o_ret = (_group_norm(of_ref[0].astype(F32), gnw_ref[...], gnb_ref[...], RET_GN_EPS) * _silu(pg[:, :w])
             + _group_norm(ob_ref[0].astype(F32), gnw_ref[...], gnb_ref[...], RET_GN_EPS) * _silu(pg[:, w:]))
    mix_scr[slot, :, :w] = o_rwkv.astype(mix_scr.dtype)
    mix_scr[slot, :, w:] = o_ret.astype(mix_scr.dtype)

    x1 = x_ref[0] + g1_ref[0] * attn
    h = _modnorm(x1, n2_ref[...], sh2_ref[0], sc2_ref[0]).astype(BF16)
    acc = None
    for c0, c1 in chunks:
        gate = _dot(h, wfi_ref[:, c0:c1])
        up = _dot(h, wfi_ref[:, hidden + c0:hidden + c1])
        part = _dot((_silu(gate) * up).astype(BF16), wfo_ref[c0:c1])
        acc = part if acc is None else acc + part
    x2 = x1 + g2_ref[0] * acc
    if final_norm:
        ms = jnp.mean(x2 * x2, axis=-1, keepdims=True)
        x2 = (x2 * lax.rsqrt(ms + NORM_EPS)) * nf_ref[...]
    o_ref[0] = x2


def _out_ffn(x, yf, yb, bonus, gate, of, ob, pg, consts, mods, final_norm, tm):
    b, l, d = x.shape
    w = yf.shape[-1]
    hidden = consts["w_ffn_out"].shape[0]
    n_tiles = hidden // MXU_TILE
    n_chunks = -(-n_tiles // FFN_CHUNK_TILES)
    bounds = [MXU_TILE * (n_tiles * i // n_chunks) for i in range(n_chunks + 1)]
    bounds[-1] = hidden
    chunks = tuple(zip(bounds[:-1], bounds[1:]))
    bm = mods[0].shape[0]
    nj = l // tm
    n_tok_tiles = b * nj
    cur = lambda t: jnp.maximum(t - 1, 0)
    nxt = lambda t: jnp.minimum(t, n_tok_tiles - 1)
    mod_map = (lambda t: (cur(t) // nj, 0, 0)) if bm > 1 else (lambda t: (0, 0, 0))
    tok_cur = lambda width: pl.BlockSpec((1, tm, width), lambda t: (cur(t) // nj, cur(t) % nj, 0))
    tok_nxt = lambda width: pl.BlockSpec((1, tm, width), lambda t: (nxt(t) // nj, nxt(t) % nj, 0))
    const = lambda arr: pl.BlockSpec(arr.shape, lambda t: (0,) * arr.ndim, pipeline_mode=pl.Buffered(1))
    names = ("ln_w", "ln_b", "gn_w", "gn_b", "w_out")
    tail = ("norm2", "norm_f", "w_ffn_in", "w_ffn_out")
    in_specs = ([tok_cur(d)] + [tok_nxt(w)] * 6 + [tok_nxt(2 * w)] + [const(consts[n]) for n in names]
                + [pl.BlockSpec((1, 1, d), mod_map)] * 4 + [const(consts[n]) for n in tail])
    args = ([x, yf, yb, bonus, gate, of, ob, pg] + [consts[n] for n in names] + list(mods)
            + [consts[n] for n in tail])
    kern = functools.partial(_out_ffn_kernel, width=w, hidden=hidden, chunks=chunks, final_norm=final_norm)
    return pl.pallas_call(
        kern,
        grid=(n_tok_tiles + 1,),
        in_specs=in_specs,
        out_specs=tok_cur(d),
        out_shape=jax.ShapeDtypeStruct((b, l, d), F32),
        scratch_shapes=[pltpu.VMEM((2, tm, 2 * w), BF16)],
        compiler_params=_params("arbitrary"),
        name="out_ffn",
    )(*args)


def _pad_rows(a, rows, offset=0):
    out = jnp.zeros((rows,) + a.shape[1:], a.dtype)
    return out.at[offset:offset + a.shape[0]].set(a)


def _layer_consts(i, w_in, w_vres_down, mu_rwkv, mu_vres, w0, w_up, a0, a_up, g_up, k_k, k_a, r_k, v0, v_up,
                  ln_x_w, ln_x_b, ret_gn_w, ret_gn_b, w_out, w_ffn_in, w_ffn_out, norm2, norm_f):
    width = w0.shape[-1]
    n_decay, n_iclr, n_gate = w_up.shape[2], a_up.shape[2], g_up.shape[1]
    n_vres = v_up.shape[1]
    n_rwkv = 3 * width + n_decay + n_iclr + n_gate
    assert n_decay + n_iclr == 128 and n_gate + n_vres <= 256
    rwkv_cols = 3 * width + 128 + 256
    d = w_in.shape[1]
    has_vres = i > 0
    wi = w_in[i]
    pad = jnp.zeros((d, rwkv_cols - n_rwkv - n_vres), F32)
    vres_cols = w_vres_down[i - 1] if has_vres else jnp.zeros((d, n_vres), F32)
    w_proj = jnp.concatenate([wi[:, :n_rwkv], vres_cols, pad, wi[:, n_rwkv:]], axis=1).astype(BF16)
    mu_v = mu_vres[i - 1] if has_vres else jnp.zeros((n_vres,), F32)
    mu = jnp.concatenate([mu_rwkv[i], mu_v, jnp.zeros((rwkv_cols - n_rwkv - n_vres,), F32)])[None]
    cat2 = lambda a: jnp.concatenate([a[0], a[1]], axis=-1)
    row = lambda a: a.reshape(1, -1)
    return dict(
        w_proj=w_proj, widths=(rwkv_cols, 3 * width, 2 * width), n_rwkv=n_rwkv, n_vres=n_vres, mu=mu,
        w0=row(cat2(w0[i])), w_up=_pad_rows(cat2(w_up[i]), 128), a0=row(cat2(a0[i])),
        a_up=_pad_rows(cat2(a_up[i]), 128, n_decay), g_up=_pad_rows(g_up[i], 256),
        v_up=_pad_rows(v_up[i - 1] if has_vres else jnp.zeros((n_vres, width), F32), 256, n_gate),
        v0=row(v0[i - 1] if has_vres else jnp.zeros((width,), F32)),
        k_k=row(k_k[i]), k_a=row(k_a[i]), r_k=row(r_k[i]),
        ln_w=row(ln_x_w[i]), ln_b=row(ln_x_b[i]), gn_w=row(ret_gn_w[i]), gn_b=row(ret_gn_b[i]),
        w_out=w_out[i].astype(BF16), w_ffn_in=w_ffn_in[i].astype(BF16), w_ffn_out=w_ffn_out[i].astype(BF16),
        norm2=row(norm2[i]), norm_f=row(norm_f),
    )


def _mixers(x, norm1, shift, scale, consts, vfirst, s_rwkv, s_ret, rope, tables, quad, tm):
    p_qkv, p_g, r, v, kk, bonus, gate, lwf, lwb, kf, kb, bf, bb = _proj_prep(
        x, norm1, shift, scale, consts, rope, vfirst, quad, tm)
    yf, yb, of, ob, s_rwkv, s_ret = _mix_scan(r, v, kk, lwf, lwb, kf, kb, bf, bb, p_qkv, tables, s_rwkv, s_ret)
    return (yf, yb, bonus, gate, of, ob, p_g), v, s_rwkv, s_ret


def kernel(x, c, ctx, c_ctx, w_ada, b_ada, norm1, norm2, norm_f, w_in, w_vres_down, mu_rwkv, mu_vres, w0, w_up, a0, a_up, g_up, k_k, k_a, r_k, v0, v_up, ln_x_w, ln_x_b, ret_gn_w, ret_gn_b, w_out, w_ffn_in, w_ffn_out):
    b, l, d = x.shape
    l_ctx = ctx.shape[1]
    depth = w_in.shape[0]
    width = w0.shape[-1]
    n_heads = width // HEAD_DIM
    n_pairs = n_heads // 2
    tm = min(TM_PROJ, l)
    tm_ctx = l_ctx
    tm_ffn = min(TM_FFN, l)
    tm_ffn_ctx = min(TM_FFN, l_ctx)

    m_pad = -(-(b + 1) // 8) * 8
    cond = jnp.zeros((m_pad, d), F32).at[:b].set(c).at[b].set(c_ctx)
    mod = _ada(cond, w_ada, b_ada)

    tables = _ret_tables(n_heads)
    rope_l = _rope_tables(l, n_heads)
    rope_c = None
    zero_ret = jnp.zeros((2, b, width // PAIR, PAIR, PAIR), F32)
    zero_rwkv = jnp.zeros((2, b, width // RWKV_GROUP, RWKV_GROUP, RWKV_GROUP), F32)

    vfirst_c = vfirst_l = None
    for i in range(depth):
        last = i == depth - 1
        consts = _layer_consts(i, w_in, w_vres_down, mu_rwkv, mu_vres, w0, w_up, a0, a_up, g_up, k_k, k_a, r_k,
                               v0, v_up, ln_x_w, ln_x_b, ret_gn_w, ret_gn_b, w_out, w_ffn_in, w_ffn_out,
                               norm2, norm_f)
        mods_l = [mod[i, :b, None, j * d:(j + 1) * d] for j in range(6)]
        mods_c = [mod[i, b:b + 1, None, j * d:(j + 1) * d] for j in range(6)]
        n1 = norm1[i][None]

        mix_c, v_c, s_rwkv, s_ret = _mixers(ctx, n1, mods_c[0], mods_c[1], consts, vfirst_c,
                                            zero_rwkv, zero_ret, rope_c, tables, False, tm_ctx)
        mix_l, v_l, _, _ = _mixers(x, n1, mods_l[0], mods_l[1], consts, vfirst_l,
                                   s_rwkv, s_ret, rope_l, tables, True, tm)
        if i == 0:
            vfirst_c, vfirst_l = v_c, v_l
        x = _out_ffn(x, *mix_l, consts, (mods_l[2], mods_l[3], mods_l[4], mods_l[5]), last, tm_ffn)
        if not last:
            ctx = _out_ffn(ctx, *mix_c, consts, (mods_c[2], mods_c[3], mods_c[4], mods_c[5]), False, tm_ffn_ctx)
    return x
```
